```python
import math
import jax, jax.numpy as jnp
from jax import lax
import numpy as np

D_MODEL = 1024
BATCH = 16
SEQ = 256
DEPTH = 2
DEC_BATCH = 8
DEC_SEQ = 4096
PAST_LEN = 512

GRID_W = 64
N_SSD = (DEPTH + 1) // 2
N_GDN = DEPTH // 2
CHUNK = 64
CONV_K = 3
EPS = 1e-6
SSD_INNER = 2 * D_MODEL
SSD_HEAD_DIM = 64
SSD_HEADS = SSD_INNER // SSD_HEAD_DIM
SSD_GROUPS = 4
SSD_HPG = SSD_HEADS // SSD_GROUPS
SSD_STATE = 128
SSD_GN = SSD_GROUPS * SSD_STATE
SSD_CONV_DIM = SSD_INNER + 2 * SSD_GN
SSD_PROJ = SSD_INNER + SSD_CONV_DIM + 2 * SSD_HEADS
GDN_HEADS = 8
GDN_DK = 128
GDN_DV = 256
GDN_QK = GDN_HEADS * GDN_DK
GDN_VW = GDN_HEADS * GDN_DV
GDN_CONV_DIM = 2 * GDN_QK + GDN_VW
GDN_PROJ = GDN_CONV_DIM + GDN_VW + 4 * GDN_HEADS
PEER_HEADS = 8
N_KEYS = 128
N_EXPERTS = N_KEYS * N_KEYS
PEER_TOPK = 16
PEER_DKEY = 256
PEER_HALF = PEER_DKEY // 2
PEER_BLOCK = 128

kernel_name = "bidir_ssd_deltanet_peer_diffusion_step"


def rmsnorm(x, g):
    xf = x.astype(jnp.float32)
    xf = xf * lax.rsqrt(jnp.mean(xf * xf, axis=-1, keepdims=True) + EPS)
    return (xf * g.astype(jnp.float32)).astype(x.dtype)


def adaln(x, g, shift, scale):
    return rmsnorm(x, g) * (1 + scale) + shift


def l2norm(x):
    return x * lax.rsqrt(jnp.sum(x * x, axis=-1, keepdims=True) + EPS)


def short_conv(x, w):
    pad = CONV_K // 2
    n = x.shape[1]
    xp = jnp.pad(x, ((0, 0), (pad, pad), (0, 0)))
    return sum(xp[:, t:t + n] * w[t] for t in range(CONV_K))


def grid_pos_embed(n, dtype):
    rows = n // GRID_W
    r = jnp.repeat(jnp.arange(rows, dtype=jnp.float32), GRID_W)
    col = jnp.tile(jnp.arange(GRID_W, dtype=jnp.float32), rows)
    quarter = D_MODEL // 4
    omega = 1.0 / (10000.0 ** (jnp.arange(quarter, dtype=jnp.float32) / quarter))
    def enc(p):
        ang = p[:, None] * omega[None, :]
        return jnp.concatenate([jnp.sin(ang), jnp.cos(ang)], axis=-1)
    return jnp.concatenate([enc(r), enc(col)], axis=-1).astype(dtype)


def ssd_scan(x, dt, a, bm, cm, s0):
    b, n, G, R, P = x.shape
    N = bm.shape[-1]
    nc = n // CHUNK
    x = x.reshape(b, nc, CHUNK, G, R, P)
    dt = dt.reshape(b, nc, CHUNK, G, R)
    bm = bm.reshape(b, nc, CHUNK, G, N)
    cm = cm.reshape(b, nc, CHUNK, G, N)
    acs = jnp.cumsum(dt * a, axis=2)
    xdt = x * dt[..., None]
    idx = jnp.arange(CHUNK)
    tril = (idx[:, None] >= idx[None, :])[:, :, None, None]
    lmat = jnp.exp(jnp.where(tril, acs[:, :, :, None] - acs[:, :, None], -jnp.inf))
    cb = jnp.einsum('bcign,bcjgn->bcijg', cm, bm)
    y_in = jnp.einsum('bcijgr,bcjgrp->bcigrp', cb[..., None] * lmat, xdt)
    xw = xdt * jnp.exp(acs[:, :, -1:] - acs)[..., None]
    e_acs = jnp.exp(acs)
    dec = jnp.exp(acs[:, :, -1])
    def step(s, inp):
        xw_c, b_c, c_c, e_c, d_c = inp
        y_c = jnp.einsum('bign,bgrpn->bigrp', c_c, s) * e_c[..., None]
        s = s * d_c[..., None, None] + jnp.einsum('bjgn,bjgrp->bgrpn', b_c, xw_c)
        return s, y_c
    seqs = tuple(jnp.moveaxis(t, 1, 0) for t in (xw, bm, cm, e_acs, dec))
    s_fin, y_x = lax.scan(step, s0, seqs)
    y = y_in + jnp.moveaxis(y_x, 0, 1)
    return y.reshape(b, n, G, R, P), s_fin


def gdn_scan(q, k, v, beta, g, s0):
    b, n, H, DK = q.shape
    DV = v.shape[-1]
    nc = n // CHUNK
    def to_chunks(t):
        t = t.reshape((b, nc, CHUNK) + t.shape[2:])
        return jnp.moveaxis(t, 2, 3)
    q, k, v, beta, g = (to_chunks(t) for t in (q, k, v, beta, g))
    gc = jnp.cumsum(g, axis=-1)
    idx = jnp.arange(CHUNK)
    incl = idx[:, None] >= idx[None, :]
    strict = idx[:, None] > idx[None, :]
    decay = jnp.exp(jnp.where(incl, gc[..., :, None] - gc[..., None, :], -jnp.inf))
    kb = k * beta[..., None]
    vb = v * beta[..., None]
    m = jnp.where(strict, jnp.einsum('bnhid,bnhjd->bnhij', kb, k) * decay, 0.0)
    eye = jnp.eye(CHUNK, dtype=m.dtype)
    tmat = lax.linalg.triangular_solve(m + eye, jnp.broadcast_to(eye, m.shape), left_side=True, lower=True)
    u = tmat @ vb
    w = tmat @ (kb * jnp.exp(gc)[..., None])
    a_qk = jnp.where(incl, jnp.einsum('bnhid,bnhjd->bnhij', q, k) * decay, 0.0)
    qg = q * jnp.exp(gc)[..., None]
    kg = k * jnp.exp(gc[..., -1:] - gc)[..., None]
    glast = jnp.exp(gc[..., -1])
    def step(s, inp):
        u_c, w_c, a_c, qg_c, kg_c, gl_c = inp
        v_new = u_c - w_c @ s
        o_c = qg_c @ s + a_c @ v_new
        s = s * gl_c[..., None, None] + jnp.swapaxes(kg_c, -1, -2) @ v_new
        return s, o_c
    seqs = tuple(jnp.moveaxis(t, 1, 0) for t in (u, w, a_qk, qg, kg, glast))
    s_fin, o = lax.scan(step, s0, seqs)
    o = jnp.moveaxis(jnp.moveaxis(o, 0, 1), 3, 2).reshape(b, n, H, DV)
    return o, s_fin


def ssd_mixer(h, w_in, conv_w, conv_b, dt_bias, a_log, d_skip, norm_g, w_out, s0_f, s0_b):
    f32 = jnp.float32
    b, n, _ = h.shape
    proj = h @ w_in
    z = proj[..., :SSD_INNER]
    xbc = proj[..., SSD_INNER:SSD_INNER + SSD_CONV_DIM]
    dt_raw = proj[..., SSD_INNER + SSD_CONV_DIM:]
    xbc = jax.nn.silu(short_conv(xbc, conv_w) + conv_b).astype(f32)
    xs = xbc[..., :SSD_INNER].reshape(b, n, SSD_GROUPS, SSD_HPG, SSD_HEAD_DIM)
    bm = xbc[..., SSD_INNER:SSD_INNER + SSD_GN].reshape(b, n, SSD_GROUPS, SSD_STATE)
    cm = xbc[..., SSD_INNER + SSD_GN:].reshape(b, n, SSD_GROUPS, SSD_STATE)
    dt = jax.nn.softplus(dt_raw.astype(f32).reshape(b, n, 2, SSD_GROUPS, SSD_HPG)
                         + dt_bias.astype(f32).reshape(2, SSD_GROUPS, SSD_HPG))
    a = -jnp.exp(a_log.astype(f32)).reshape(2, SSD_GROUPS, SSD_HPG)
    st = (b, SSD_GROUPS, SSD_HPG, SSD_HEAD_DIM, SSD_STATE)
    y_f, s_f = ssd_scan(xs, dt[:, :, 0], a[0], bm, cm, s0_f.astype(f32).reshape(st))
    y_b, s_b = ssd_scan(xs[:, ::-1], dt[:, ::-1, 1], a[1], bm[:, ::-1], cm[:, ::-1], s0_b.astype(f32).reshape(st))
    y = y_f + y_b[:, ::-1] + xs * d_skip.astype(f32).reshape(SSD_GROUPS, SSD_HPG, 1)
    y = rmsnorm(y.reshape(b, n, SSD_INNER) * jax.nn.silu(z.astype(f32)), norm_g)
    out = y.astype(h.dtype) @ w_out
    hs = (b, SSD_HEADS, SSD_HEAD_DIM, SSD_STATE)
    return out, s_f.reshape(hs).astype(h.dtype), s_b.reshape(hs).astype(h.dtype)


def gdn_mixer(h, w_in, conv_w, dt_bias, a_log, norm_g, w_out, s0_f, s0_b):
    f32 = jnp.float32
    b, n, _ = h.shape
    proj = h @ w_in
    qkv = jax.nn.silu(short_conv(proj[..., :GDN_CONV_DIM], conv_w)).astype(f32)
    z = proj[..., GDN_CONV_DIM:GDN_CONV_DIM + GDN_VW].astype(f32).reshape(b, n, GDN_HEADS, GDN_DV)
    ab = proj[..., GDN_CONV_DIM + GDN_VW:].astype(f32).reshape(b, n, 2, 2, GDN_HEADS)
    q = l2norm(qkv[..., :GDN_QK].reshape(b, n, GDN_HEADS, GDN_DK)) * (GDN_DK ** -0.5)
    k = l2norm(qkv[..., GDN_QK:2 * GDN_QK].reshape(b, n, GDN_HEADS, GDN_DK))
    v = qkv[..., 2 * GDN_QK:].reshape(b, n, GDN_HEADS, GDN_DV)
    beta = jax.nn.sigmoid(ab[:, :, 0])
    g = -jnp.exp(a_log.astype(f32)) * jax.nn.softplus(ab[:, :, 1] + dt_bias.astype(f32))
    o_f, s_f = gdn_scan(q, k, v, beta[:, :, 0], g[:, :, 0], s0_f.astype(f32))
    o_b, s_b = gdn_scan(q[:, ::-1], k[:, ::-1], v[:, ::-1], beta[:, ::-1, 1], g[:, ::-1, 1], s0_b.astype(f32))
    o = rmsnorm(o_f + o_b[:, ::-1], norm_g) * jax.nn.silu(z)
    out = o.reshape(b, n, GDN_VW).astype(h.dtype) @ w_out
    return out, s_f.astype(h.dtype), s_b.astype(h.dtype)


def peer(h, w_q, keys, u_tab, v_tab):
    b, n, D = h.shape
    xt = h.reshape(-1, PEER_BLOCK, D)
    def block(xb):
        t = xb.shape[0]
        q = (xb @ w_q).reshape(t, PEER_HEADS, 2, PEER_HALF)
        s = jnp.einsum('thsk,shnk->thsn', q, keys).astype(jnp.float32)
        sv, si = lax.top_k(s, PEER_TOPK)
        comb = (sv[:, :, 0, :, None] + sv[:, :, 1, None, :]).reshape(t, PEER_HEADS, PEER_TOPK * PEER_TOPK)
        cv, ci = lax.top_k(comb, PEER_TOPK)
        i1 = jnp.take_along_axis(si[:, :, 0], ci // PEER_TOPK, axis=-1)
        i2 = jnp.take_along_axis(si[:, :, 1], ci % PEER_TOPK, axis=-1)
        eid = (i1 * N_KEYS + i2).reshape(t, PEER_HEADS * PEER_TOPK)
        gate = jax.nn.softmax(cv, axis=-1).reshape(t, PEER_HEADS * PEER_TOPK)
        u = jnp.take(u_tab, eid, axis=0)
        act = jax.nn.gelu(jnp.einsum('td,ted->te', xb, u).astype(jnp.float32), approximate=False) * gate
        vv = jnp.take(v_tab, eid, axis=0)
        return jnp.einsum('te,ted->td', act.astype(xb.dtype), vv)
    return lax.map(block, xt).reshape(b, n, D)


def setup_inputs(seed: int = 0) -> dict:
    key = jax.random.key(seed)
    ks = jax.random.split(key, 32)
    D = D_MODEL
    def nrm(i, shape, s):
        return jax.random.normal(ks[i], shape, jnp.float32) * s
    def uni(i, shape, lo, hi):
        return jax.random.uniform(ks[i], shape, jnp.float32, lo, hi)
    def dt_bias(i, shape):
        dt = jnp.exp(uni(i, shape, math.log(1e-3), math.log(1e-1)))
        return dt + jnp.log(-jnp.expm1(-dt))
    ssd_st = (DEC_BATCH, N_SSD, SSD_HEADS, SSD_HEAD_DIM, SSD_STATE)
    gdn_st = (DEC_BATCH, N_GDN, GDN_HEADS, GDN_DK, GDN_DV)
    return {
        "x_prompt": nrm(0, (BATCH, SEQ, D), 1.0),
        "x_sample": nrm(1, (DEC_BATCH, DEC_SEQ, D), 1.0),
        "c": nrm(2, (DEC_BATCH, D), 1.0),
        "state_ssd_fwd": nrm(3, ssd_st, 0.02),
        "state_ssd_bwd": nrm(4, ssd_st, 0.02),
        "state_gdn_fwd": nrm(5, gdn_st, 0.1),
        "state_gdn_bwd": nrm(6, gdn_st, 0.1),
        "c_ctx": nrm(7, (D,), 1.0),
        "w_mod": nrm(8, (DEPTH, D, 6 * D), 0.5 * D ** -0.5),
        "b_mod": nrm(9, (DEPTH, 6 * D), 0.02),
        "norm_mix_g": 1.0 + nrm(10, (DEPTH, D), 0.02),
        "norm_ffn_g": 1.0 + nrm(11, (DEPTH, D), 0.02),
        "ssd_w_in": nrm(12, (N_SSD, D, SSD_PROJ), D ** -0.5),
        "ssd_conv_w": nrm(13, (N_SSD, CONV_K, SSD_CONV_DIM), CONV_K ** -0.5),
        "ssd_conv_b": nrm(14, (N_SSD, SSD_CONV_DIM), 0.02),
        "ssd_dt_bias": dt_bias(15, (N_SSD, 2, SSD_HEADS)),
        "ssd_a_log": jnp.log(uni(16, (N_SSD, 2, SSD_HEADS), 1.0, 16.0)),
        "ssd_d": 1.0 + nrm(17, (N_SSD, SSD_HEADS), 0.02),
        "ssd_norm_g": 1.0 + nrm(18, (N_SSD, SSD_INNER), 0.02),
        "ssd_w_out": nrm(19, (N_SSD, SSD_INNER, D), SSD_INNER ** -0.5),
        "gdn_w_in": nrm(20, (N_GDN, D, GDN_PROJ), D ** -0.5),
        "gdn_conv_w": nrm(21, (N_GDN, CONV_K, GDN_CONV_DIM), CONV_K ** -0.5),
        "gdn_dt_bias": dt_bias(22, (N_GDN, 2, GDN_HEADS)),
        "gdn_a_log": jnp.log(uni(23, (N_GDN, 2, GDN_HEADS), 1.0, 16.0)),
        "gdn_norm_g": 1.0 + nrm(24, (N_GDN, GDN_DV), 0.02),
        "gdn_w_out": nrm(25, (N_GDN, GDN_VW, D), GDN_VW ** -0.5),
        "peer_w_q": nrm(26, (DEPTH, D, PEER_HEADS * PEER_DKEY), D ** -0.5),
        "peer_keys": nrm(27, (DEPTH, 2, PEER_HEADS, N_KEYS, PEER_HALF), PEER_HALF ** -0.5),
        "peer_u": nrm(28, (DEPTH, N_EXPERTS, D), D ** -0.5),
        "peer_v": nrm(29, (DEPTH, N_EXPERTS, D), 0.5),
        "final_norm_g": 1.0 + nrm(30, (D,), 0.02),
    }


def reference(x_prompt, x_sample, c, state_ssd_fwd, state_ssd_bwd, state_gdn_fwd, state_gdn_bwd, c_ctx,
              w_mod, b_mod, norm_mix_g, norm_ffn_g,
              ssd_w_in, ssd_conv_w, ssd_conv_b, ssd_dt_bias, ssd_a_log, ssd_d, ssd_norm_g, ssd_w_out,
              gdn_w_in, gdn_conv_w, gdn_dt_bias, gdn_a_log, gdn_norm_g, gdn_w_out,
              peer_w_q, peer_keys, peer_u, peer_v, final_norm_g):
    xp = x_prompt
    xs = x_sample + grid_pos_embed(x_sample.shape[1], x_sample.dtype)[None]
    bp = xp.shape[0]
    ssd_f, ssd_b, gdn_f, gdn_b = [], [], [], []
    for i in range(DEPTH):
        mod_p = jnp.split((jax.nn.silu(c_ctx) @ w_mod[i] + b_mod[i])[None, None, :], 6, axis=-1)
        mod_s = jnp.split((jax.nn.silu(c) @ w_mod[i] + b_mod[i])[:, None, :], 6, axis=-1)
        hp = adaln(xp, norm_mix_g[i], mod_p[0], mod_p[1])
        hs = adaln(xs, norm_mix_g[i], mod_s[0], mod_s[1])
        j = i // 2
        if i % 2 == 0:
            prm = (ssd_w_in[j], ssd_conv_w[j], ssd_conv_b[j], ssd_dt_bias[j], ssd_a_log[j], ssd_d[j], ssd_norm_g[j], ssd_w_out[j])
            z0 = jnp.zeros((bp, SSD_HEADS, SSD_HEAD_DIM, SSD_STATE), xp.dtype)
            mp, sf, sb = ssd_mixer(hp, *prm, z0, z0)
            ms, _, _ = ssd_mixer(hs, *prm, state_ssd_fwd[:, j], state_ssd_bwd[:, j])
            ssd_f.append(sf)
            ssd_b.append(sb)
        else:
            prm = (gdn_w_in[j], gdn_conv_w[j], gdn_dt_bias[j], gdn_a_log[j], gdn_norm_g[j], gdn_w_out[j])
            z0 = jnp.zeros((bp, GDN_HEADS, GDN_DK, GDN_DV), xp.dtype)
            mp, sf, sb = gdn_mixer(hp, *prm, z0, z0)
            ms, _, _ = gdn_mixer(hs, *prm, state_gdn_fwd[:, j], state_gdn_bwd[:, j])
            gdn_f.append(sf)
            gdn_b.append(sb)
        xp = xp + mod_p[2] * mp
        xs = xs + mod_s[2] * ms
        hp = adaln(xp, norm_ffn_g[i], mod_p[3], mod_p[4])
        hs = adaln(xs, norm_ffn_g[i], mod_s[3], mod_s[4])
        xp = xp + mod_p[5] * peer(hp, peer_w_q[i], peer_keys[i], peer_u[i], peer_v[i])
        xs = xs + mod_s[5] * peer(hs, peer_w_q[i], peer_keys[i], peer_u[i], peer_v[i])
    y_prompt = rmsnorm(xp, final_norm_g)
    y_sample = rmsnorm(xs, final_norm_g)
    new_state_ssd_fwd = jnp.stack(ssd_f, axis=1)
    new_state_ssd_bwd = jnp.stack(ssd_b, axis=1)
    new_state_gdn_fwd = jnp.stack(gdn_f, axis=1)
    new_state_gdn_bwd = jnp.stack(gdn_b, axis=1)
    return (y_prompt, y_sample, new_state_ssd_fwd, new_state_ssd_bwd, new_state_gdn_fwd, new_state_gdn_bwd)
```

```python
import functools
import math

import jax
import jax.numpy as jnp
from jax import lax
from jax.experimental import pallas as pl
from jax.experimental.pallas import tpu as pltpu

F32 = jnp.float32
BF16 = jnp.bfloat16
I32 = jnp.int32
HIGHEST = lax.Precision.HIGHEST

D_MODEL = 1024
DEPTH = 2
GRID_W = 64
CHUNK = 64
EPS = 1e-6
SSD_INNER = 2 * D_MODEL
SSD_HEAD_DIM = 64
SSD_HEADS = SSD_INNER // SSD_HEAD_DIM
SSD_GROUPS = 4
SSD_HPG = SSD_HEADS // SSD_GROUPS
SSD_STATE = 128
SSD_GN = SSD_GROUPS * SSD_STATE
SSD_CONV_DIM = SSD_INNER + 2 * SSD_GN
GDN_HEADS = 8
GDN_DK = 128
GDN_DV = 256
GDN_QK = GDN_HEADS * GDN_DK
GDN_VW = GDN_HEADS * GDN_DV
GDN_CONV_DIM = 2 * GDN_QK + GDN_VW
GDN_HPG = 4
GDN_GROUPS = GDN_HEADS // GDN_HPG
PEER_HEADS = 8
N_KEYS = 128
N_EXPERTS = N_KEYS * N_KEYS
PEER_TOPK = 16
PEER_DKEY = 256
PEER_HALF = PEER_DKEY // 2
PEER_SLOTS = PEER_HEADS * PEER_TOPK

LANES = 128
SUBLANES = 8
ROW_VREGS = D_MODEL // LANES
HALF_ROWS = ROW_VREGS // 2
VMEM_LIMIT = 56 * 1024 * 1024


def _cparams(sem):
    return pltpu.CompilerParams(dimension_semantics=sem, vmem_limit_bytes=VMEM_LIMIT)


def _const_spec(shape):
    nd = len(shape)
    return pl.BlockSpec(shape, lambda *_: (0,) * nd, pipeline_mode=pl.Buffered(1))


def _mod_spec(bm):
    if bm == 1:
        return pl.BlockSpec((1, 1, D_MODEL), lambda b, *_: (0, 0, 0))
    return pl.BlockSpec((1, 1, D_MODEL), lambda b, *_: (b, 0, 0))


def _silu(x):
    return x * jax.nn.sigmoid(x)


def _split_bf16(a):
    hi = a.astype(BF16)
    lo = (a - hi.astype(F32)).astype(BF16)
    return hi, lo


def _dot(a, b):
    return jnp.dot(a, b, preferred_element_type=F32)


def _dot_nt(a, b):
    return lax.dot_general(a, b, (((1,), (1,)), ((), ())), preferred_element_type=F32)


def _dot3(a, b):
    ah, al = _split_bf16(a)
    bh, bl = _split_bf16(b)
    return _dot(ah, bh) + _dot(ah, bl) + _dot(al, bh)


def _adaln(x, g, shift, scale):
    ms = jnp.mean(x * x, axis=-1, keepdims=True)
    return x * lax.rsqrt(ms + EPS) * g * (1.0 + scale) + shift


def _mod_body(c_ref, w_ref, b_ref, o_ref):
    s = _silu(c_ref[...])
    o_ref[0] = jnp.dot(s, w_ref[0], precision=HIGHEST, preferred_element_type=F32) + b_ref[0]


def _modulation(c_all, w_mod, b_mod):
    depth, d, n = w_mod.shape
    rows = c_all.shape[0]
    tn = 1536
    return pl.pallas_call(
        _mod_body,
        grid=(depth, n // tn),
        in_specs=[
            pl.BlockSpec((rows, d), lambda i, j: (0, 0)),
            pl.BlockSpec((1, d, tn), lambda i, j: (i, 0, j)),
            pl.BlockSpec((1, 1, tn), lambda i, j: (i, 0, j)),
        ],
        out_specs=pl.BlockSpec((1, rows, tn), lambda i, j: (i, 0, j)),
        out_shape=jax.ShapeDtypeStruct((depth, rows, n), F32),
        compiler_params=_cparams(("arbitrary", "arbitrary")),
        name="modulation",
    )(c_all, w_mod, b_mod.reshape(depth, 1, n))


def _grid_pos_embed(n):
    rows = n // GRID_W
    r = jnp.repeat(jnp.arange(rows, dtype=F32), GRID_W)
    col = jnp.tile(jnp.arange(GRID_W, dtype=F32), rows)
    quarter = D_MODEL // 4
    omega = 1.0 / (10000.0 ** (jnp.arange(quarter, dtype=F32) / quarter))

    def enc(p):
        ang = p[:, None] * omega[None, :]
        return jnp.concatenate([jnp.sin(ang), jnp.cos(ang)], axis=-1)

    return jnp.concatenate([enc(r), enc(col)], axis=-1)


def _add_body(x_ref, p_ref, o_ref):
    o_ref[0] = x_ref[0] + p_ref[...]


def _add_pos(x, pe):
    b, l, d = x.shape
    tl = 512
    return pl.pallas_call(
        _add_body,
        grid=(l // tl, b),
        in_specs=[pl.BlockSpec((1, tl, d), lambda i, j: (j, i, 0)),
                  pl.BlockSpec((tl, d), lambda i, j: (i, 0))],
        out_specs=pl.BlockSpec((1, tl, d), lambda i, j: (j, i, 0)),
        out_shape=jax.ShapeDtypeStruct(x.shape, F32),
        compiler_params=_cparams(("arbitrary", "arbitrary")),
        name="add_pos",
    )(x, pe)


def _norm_proj_body(x_ref, g_ref, sh_ref, sc_ref, *refs, n_w):
    h = _adaln(x_ref[0], g_ref[...], sh_ref[0], sc_ref[0]).astype(BF16)
    for w_ref, o_ref in zip(refs[:n_w], refs[n_w:]):
        o_ref[0] = _dot(h, w_ref[...])


def _norm_proj(x, g, shift, scale, ws):
    b, l, d = x.shape
    tl = min(l, 256)
    bm = shift.shape[0]
    in_specs = [
        pl.BlockSpec((1, tl, d), lambda i, j: (i, j, 0)),
        _const_spec((1, d)),
        _mod_spec(bm),
        _mod_spec(bm),
    ] + [_const_spec(w.shape) for w in ws]
    return pl.pallas_call(
        functools.partial(_norm_proj_body, n_w=len(ws)),
        grid=(b, l // tl),
        in_specs=in_specs,
        out_specs=[pl.BlockSpec((1, tl, w.shape[1]), lambda i, j: (i, j, 0)) for w in ws],
        out_shape=[jax.ShapeDtypeStruct((b, l, w.shape[1]), F32) for w in ws],
        compiler_params=_cparams(("arbitrary", "arbitrary")),
        name="norm_proj",
    )(x, g.reshape(1, d), shift, scale, *ws)


def _conv_body(x_ref, xm_ref, xp_ref, w_ref, b_ref, o_ref, *, n_l2, n_qscale, hd):
    i = pl.program_id(1)
    j = pl.program_id(2)
    x = x_ref[0]
    r, tc = x.shape
    row = lax.broadcasted_iota(I32, (r, tc), 0)
    prev_row = jnp.where(i == 0, 0.0, xm_ref[0][SUBLANES - 1:SUBLANES, :])
    next_row = jnp.where(i == pl.num_programs(1) - 1, 0.0, xp_ref[0][0:1, :])
    x_prev = jnp.where(row == 0, prev_row, pltpu.roll(x, 1, axis=0))
    x_next = jnp.where(row == r - 1, next_row, pltpu.roll(x, r - 1, axis=0))
    w = w_ref[...]
    y = _silu(x_prev * w[0:1] + x * w[1:2] + x_next * w[2:3] + b_ref[...])
    if n_l2 == 0:
        o_ref[0] = y
        return

    @pl.when(j >= n_l2)
    def _():
        o_ref[0] = y

    @pl.when(j < n_l2)
    def _():
        qs = jnp.where(j < n_qscale, hd ** -0.5, 1.0)
        for k in range(tc // hd):
            yk = y[:, k * hd:(k + 1) * hd]
            ss = jnp.sum(yk * yk, axis=-1, keepdims=True)
            o_ref[0, :, k * hd:(k + 1) * hd] = yk * (lax.rsqrt(ss + EPS) * qs)


def _conv_silu(x, w, bias, n_l2=0, n_qscale=0, hd=GDN_DK):
    b, l, c = x.shape
    r = min(l, 512)
    tc = 512
    rs = r // SUBLANES
    last = l // SUBLANES - 1
    return pl.pallas_call(
        functools.partial(_conv_body, n_l2=n_l2, n_qscale=n_qscale, hd=hd),
        grid=(b, l // r, c // tc),
        in_specs=[
            pl.BlockSpec((1, r, tc), lambda bi, i, j: (bi, i, j)),
            pl.BlockSpec((1, SUBLANES, tc), lambda bi, i, j: (bi, jnp.maximum(i * rs - 1, 0), j)),
            pl.BlockSpec((1, SUBLANES, tc), lambda bi, i, j: (bi, jnp.minimum((i + 1) * rs, last), j)),
            pl.BlockSpec((3, tc), lambda bi, i, j: (0, j)),
            pl.BlockSpec((1, tc), lambda bi, i, j: (0, j)),
        ],
        out_specs=pl.BlockSpec((1, r, tc), lambda bi, i, j: (bi, i, j)),
        out_shape=jax.ShapeDtypeStruct(x.shape, F32),
        compiler_params=_cparams(("arbitrary", "arbitrary", "arbitrary")),
        name="conv_silu",
    )(x, x, x, w, bias.reshape(1, c))


def _chunk_masks(fwd):
    ii = lax.broadcasted_iota(I32, (CHUNK, CHUNK), 0)
    jj = lax.broadcasted_iota(I32, (CHUNK, CHUNK), 1)
    incl = (ii >= jj) if fwd else (ii <= jj)
    strict = (ii > jj) if fwd else (ii < jj)
    return incl, strict


def _cumsum_chunk(incl, v):
    return jnp.dot(incl.astype(F32), v, precision=HIGHEST, preferred_element_type=F32)


def _softplus(x):
    return jnp.maximum(x, 0.0) + jnp.log1p(jnp.exp(-jnp.abs(x)))


def _ssd_dir(x_ref, b_ref, c_ref, dt_ref, bias, a, s_ref, y_ref, col0, fwd):
    incl, _ = _chunk_masks(fwd)
    x = x_ref[0]
    bm = b_ref[0].astype(BF16)
    cm = c_ref[0].astype(BF16)
    dt = _softplus(dt_ref[0] + bias)
    acs = _cumsum_chunk(incl, dt * a)
    acs_t = acs.T
    tot = acs[CHUNK - 1:CHUNK] if fwd else acs[0:1]
    e_acs = jnp.exp(acs)
    e_rest = jnp.exp(tot - acs)
    dec = jnp.exp(tot)
    cb = _dot_nt(cm, bm)
    p = SSD_HEAD_DIM
    for k in range(SSD_HPG):
        cc = col0 + k
        lm = jnp.exp(jnp.where(incl, acs[:, cc:cc + 1] - acs_t[cc:cc + 1, :], -jnp.inf))
        xdt = x[:, k * p:(k + 1) * p] * dt[:, cc:cc + 1]
        sk = s_ref[k]
        y_in = _dot((cb * lm).astype(BF16), xdt.astype(BF16))
        y_x = _dot_nt(cm, sk.astype(BF16)) * e_acs[:, cc:cc + 1]
        y_ref[0, :, k * p:(k + 1) * p] = y_in + y_x
        xw = (xdt * e_rest[:, cc:cc + 1]).T.astype(BF16)
        s_ref[k] = sk * dec[:, cc:cc + 1] + _dot(xw, bm)


def _ssd_body(*refs, has_state):
    (xf, bf, cf, dtf, xb, bb, cbk, dtb, bias_ref, alog_ref), refs = refs[:10], refs[10:]
    if has_state:
        (s0f, s0b), refs = refs[:2], refs[2:]
    yf, yb, sfo, sbo, sf, sb = refs
    c = pl.program_id(2)

    @pl.when(c == 0)
    def _():
        if has_state:
            sf[...] = s0f[0]
            sb[...] = s0b[0]
        else:
            sf[...] = jnp.zeros_like(sf)
            sb[...] = jnp.zeros_like(sb)

    bias = bias_ref[0]
    a = -jnp.exp(alog_ref[0])
    _ssd_dir(xf, bf, cf, dtf, bias, a, sf, yf, 0, True)
    _ssd_dir(xb, bb, cbk, dtb, bias, a, sb, yb, SSD_HPG, False)

    @pl.when(c == pl.num_programs(2) - 1)
    def _():
        sfo[0] = sf[...]
        sbo[0] = sb[...]


def _ssd_scan(xbc, dt, bias, alog, s0f=None, s0b=None):
    b, l, _ = xbc.shape
    nc = l // CHUNK
    gw = SSD_HPG * SSD_HEAD_DIM
    boff = SSD_INNER // SSD_STATE
    coff = boff + SSD_GROUPS
    has_state = s0f is not None

    def seq_specs(ci):
        return [
            pl.BlockSpec((1, CHUNK, gw), lambda bi, g, c: (bi, ci(c), g)),
            pl.BlockSpec((1, CHUNK, SSD_STATE), lambda bi, g, c: (bi, ci(c), boff + g)),
            pl.BlockSpec((1, CHUNK, SSD_STATE), lambda bi, g, c: (bi, ci(c), coff + g)),
            pl.BlockSpec((1, CHUNK, LANES), lambda bi, g, c: (bi, ci(c), g)),
        ]

    fw = lambda c: c
    bw = lambda c: nc - 1 - c
    st_spec = pl.BlockSpec((1, SSD_HPG, SSD_HEAD_DIM, SSD_STATE), lambda bi, g, c: (bi, g, 0, 0))
    par_spec = pl.BlockSpec((1, 1, LANES), lambda bi, g, c: (g, 0, 0))
    in_specs = seq_specs(fw) + seq_specs(bw) + [par_spec, par_spec]
    args = [xbc, xbc, xbc, dt, xbc, xbc, xbc, dt, bias, alog]
    if has_state:
        in_specs += [st_spec, st_spec]
        args += [s0f, s0b]
    st_shape = jax.ShapeDtypeStruct((b, SSD_HEADS, SSD_HEAD_DIM, SSD_STATE), F32)
    y_shape = jax.ShapeDtypeStruct((b, l, SSD_INNER), F32)
    return pl.pallas_call(
        functools.partial(_ssd_body, has_state=has_state),
        grid=(b, SSD_GROUPS, nc),
        in_specs=in_specs,
        out_specs=[
            pl.BlockSpec((1, CHUNK, gw), lambda bi, g, c: (bi, c, g)),
            pl.BlockSpec((1, CHUNK, gw), lambda bi, g, c: (bi, nc - 1 - c, g)),
            st_spec, st_spec,
        ],
        out_shape=[y_shape, y_shape, st_shape, st_shape],
        scratch_shapes=[pltpu.VMEM((SSD_HPG, SSD_HEAD_DIM, SSD_STATE), F32)] * 2,
        compiler_params=_cparams(("arbitrary", "arbitrary", "arbitrary")),
        name="ssd_scan",
    )(*args)


def _ssd_out_body(yf_ref, yb_ref, xs_ref, z_ref, d_ref, ng_ref, w_ref, x_ref, gate_ref, o_ref):
    y = yf_ref[0] + yb_ref[0] + xs_ref[0] * d_ref[...]
    y = y * _silu(z_ref[0])
    y = y * lax.rsqrt(jnp.mean(y * y, axis=-1, keepdims=True) + EPS) * ng_ref[...]
    o_ref[0] = x_ref[0] + gate_ref[0] * _dot(y.astype(BF16), w_ref[...])


def _ssd_out(yf, yb, xbc, z, d_skip, norm_g, w_out, x, gate):
    b, l, d = x.shape
    tl = min(l, 256)
    inner = SSD_INNER
    tok = lambda n: pl.BlockSpec((1, tl, n), lambda i, j: (i, j, 0))
    return pl.pallas_call(
        _ssd_out_body,
        grid=(b, l // tl),
        in_specs=[tok(inner), tok(inner), tok(inner), tok(inner),
                  _const_spec((1, inner)), _const_spec((1, inner)), _const_spec(w_out.shape),
                  tok(d), _mod_spec(gate.shape[0])],
        out_specs=tok(d),
        out_shape=jax.ShapeDtypeStruct(x.shape, F32),
        compiler_params=_cparams(("arbitrary", "arbitrary")),
        name="ssd_out",
    )(yf, yb, xbc, z, d_skip, norm_g, w_out, x, gate)


_GDN_PASSES = dict(kk=1, inv=3, u=1, w=1, qk=1, ws=1, qs=1, av=1, kv=1)


def _mm(a, b, passes):
    if passes == 1:
        return _dot(a.astype(BF16), b.astype(BF16))
    return _dot3(a, b)


def _mm_nt(a, b, passes):
    if passes == 1:
        return _dot_nt(a.astype(BF16), b.astype(BF16))
    ah, al = _split_bf16(a)
    bh, bl = _split_bf16(b)
    return _dot_nt(ah, bh) + _dot_nt(ah, bl) + _dot_nt(al, bh)


INV_BASE_LOG2 = 3


def _unit_tri_inverse(m, passes):
    ii = lax.broadcasted_iota(I32, (CHUNK, CHUNK), 0)
    jj = lax.broadcasted_iota(I32, (CHUNK, CHUNK), 1)
    d = jnp.where((ii >> INV_BASE_LOG2) == (jj >> INV_BASE_LOG2), m, 0.0)
    x = jnp.where(ii == jj, 1.0, 0.0) - d
    pw = d
    for _ in range(INV_BASE_LOG2 - 1):
        pw = _mm(pw, pw, passes)
        x = x + _mm(x, pw, passes)
    for lg in range(INV_BASE_LOG2, int(math.log2(CHUNK))):
        off_diag = ((ii >> (lg + 1)) == (jj >> (lg + 1))) & ((ii >> lg) != (jj >> lg))
        c = jnp.where(off_diag, m, 0.0)
        x = x - _mm(x, _mm(c, x, passes), passes)
    return x


def _gdn_dir(q_ref, k_ref, v_ref, ab_ref, bias, neg_a, s_ref, o_ref, off, fwd):
    ps = _GDN_PASSES
    incl, strict = _chunk_masks(fwd)
    q = q_ref[0]
    k = k_ref[0]
    v = v_ref[0]
    ab = ab_ref[0]
    beta_all = jax.nn.sigmoid(ab)
    g_all = neg_a * _softplus(ab + bias)
    gc = _cumsum_chunk(incl, g_all)
    gc_t = gc.T
    tot = gc[CHUNK - 1:CHUNK] if fwd else gc[0:1]
    e_gc = jnp.exp(gc)
    e_rest = jnp.exp(tot - gc)
    g_last = jnp.exp(tot)
    dk, dv = GDN_DK, GDN_DV
    for h in range(GDN_HPG):
        cb = off + h
        ca = 2 * GDN_HPG + off + h
        qh = q[:, h * dk:(h + 1) * dk]
        kh = k[:, h * dk:(h + 1) * dk]
        vh = v[:, h * dv:(h + 1) * dv]
        beta = beta_all[:, cb:cb + 1]
        decay = jnp.exp(jnp.where(incl, gc[:, ca:ca + 1] - gc_t[ca:ca + 1, :], -jnp.inf))
        kb = kh * beta
        mm = jnp.where(strict, _mm_nt(kb, kh, ps["kk"]) * decay, 0.0)
        t = _unit_tri_inverse(mm, ps["inv"])
        u = _mm(t, vh * beta, ps["u"])
        w = _mm(t, kb * e_gc[:, ca:ca + 1], ps["w"])
        a_qk = jnp.where(incl, _mm_nt(qh, kh, ps["qk"]) * decay, 0.0)
        sh = s_ref[h]
        v_new = u - _mm(w, sh, ps["ws"])
        o = _mm(qh * e_gc[:, ca:ca + 1], sh, ps["qs"]) + _mm(a_qk, v_new, ps["av"])
        o_ref[0, :, h * dv:(h + 1) * dv] = o
        kg_t = (kh * e_rest[:, ca:ca + 1]).T
        s_ref[h] = sh * g_last[:, ca:ca + 1] + _mm(kg_t, v_new, ps["kv"])


def _gdn_body(*refs, has_state):
    (qf, kf, vf, abf, qb, kb, vb, abb, bias_ref, alog_ref), refs = refs[:10], refs[10:]
    if has_state:
        (s0f, s0b), refs = refs[:2], refs[2:]
    of, ob, sfo, sbo, sf, sb = refs
    c = pl.program_id(2)

    @pl.when(c == 0)
    def _():
        if has_state:
            sf[...] = s0f[0]
            sb[...] = s0b[0]
        else:
            sf[...] = jnp.zeros_like(sf)
            sb[...] = jnp.zeros_like(sb)

    bias = bias_ref[0]
    neg_a = -jnp.exp(alog_ref[0])
    _gdn_dir(qf, kf, vf, abf, bias, neg_a, sf, of, 0, True)
    _gdn_dir(qb, kb, vb, abb, bias, neg_a, sb, ob, GDN_HPG, False)

    @pl.when(c == pl.num_programs(2) - 1)
    def _():
        sfo[0] = sf[...]
        sbo[0] = sb[...]


def _gdn_scan(qkv, ab, bias, alog, s0f=None, s0b=None):
    b, l, _ = qkv.shape
    nc = l // CHUNK
    qw = GDN_HPG * GDN_DK
    vw = GDN_HPG * GDN_DV
    koff = GDN_QK // qw
    voff = 2 * GDN_QK // vw
    has_state = s0f is not None

    def seq_specs(ci):
        return [
            pl.BlockSpec((1, CHUNK, qw), lambda bi, g, c: (bi, ci(c), g)),
            pl.BlockSpec((1, CHUNK, qw), lambda bi, g, c: (bi, ci(c), koff + g)),
            pl.BlockSpec((1, CHUNK, vw), lambda bi, g, c: (bi, ci(c), voff + g)),
            pl.BlockSpec((1, CHUNK, LANES), lambda bi, g, c: (bi, ci(c), g)),
        ]

    fw = lambda c: c
    bw = lambda c: nc - 1 - c
    st_spec = pl.BlockSpec((1, GDN_HPG, GDN_DK, GDN_DV), lambda bi, g, c: (bi, g, 0, 0))
    par_spec = pl.BlockSpec((1, 1, LANES), lambda bi, g, c: (g, 0, 0))
    in_specs = seq_specs(fw) + seq_specs(bw) + [par_spec, par_spec]
    args = [qkv, qkv, qkv, ab, qkv, qkv, qkv, ab, bias, alog]
    if has_state:
        in_specs += [st_spec, st_spec]
        args += [s0f, s0b]
    st_shape = jax.ShapeDtypeStruct((b, GDN_HEADS, GDN_DK, GDN_DV), F32)
    o_shape = jax.ShapeDtypeStruct((b, l, GDN_VW), F32)
    return pl.pallas_call(
        functools.partial(_gdn_body, has_state=has_state),
        grid=(b, GDN_GROUPS, nc),
        in_specs=in_specs,
        out_specs=[
            pl.BlockSpec((1, CHUNK, vw), lambda bi, g, c: (bi, c, g)),
            pl.BlockSpec((1, CHUNK, vw), lambda bi, g, c: (bi, nc - 1 - c, g)),
            st_spec, st_spec,
        ],
        out_shape=[o_shape, o_shape, st_shape, st_shape],
        scratch_shapes=[pltpu.VMEM((GDN_HPG, GDN_DK, GDN_DV), F32)] * 2,
        compiler_params=_cparams(("arbitrary", "arbitrary", "arbitrary")),
        name="gdn_scan",
    )(*args)


def _gdn_out_body(of_ref, ob_ref, z_ref, ng_ref, w_ref, x_ref, gate_ref, o_ref, y_ref):
    dv = GDN_DV
    for h in range(GDN_HEADS):
        sl = slice(h * dv, (h + 1) * dv)
        o = of_ref[0, :, sl] + ob_ref[0, :, sl]
        o = o * lax.rsqrt(jnp.mean(o * o, axis=-1, keepdims=True) + EPS) * ng_ref[...]
        y_ref[:, sl] = (o * _silu(z_ref[0, :, sl])).astype(BF16)
    o_ref[0] = x_ref[0] + gate_ref[0] * _dot(y_ref[...], w_ref[...])


def _gdn_out(of, ob, z, norm_g, w_out, x, gate):
    b, l, d = x.shape
    tl = min(l, 256)
    tok = lambda n: pl.BlockSpec((1, tl, n), lambda i, j: (i, j, 0))
    return pl.pallas_call(
        _gdn_out_body,
        grid=(b, l // tl),
        in_specs=[tok(GDN_VW), tok(GDN_VW), tok(GDN_VW),
                  _const_spec((1, GDN_DV)), _const_spec(w_out.shape),
                  tok(d), _mod_spec(gate.shape[0])],
        out_specs=tok(d),
        out_shape=jax.ShapeDtypeStruct(x.shape, F32),
        scratch_shapes=[pltpu.VMEM((tl, GDN_VW), BF16)],
        compiler_params=_cparams(("arbitrary", "arbitrary")),
        name="gdn_out",
    )(of, ob, z, norm_g, w_out, x, gate)


def _topk_rows(s, k):
    n = s.shape[0]
    iota = lax.broadcasted_iota(I32, s.shape, 0)
    vals, idxs = [], []
    for _ in range(k):
        m = jnp.max(s, axis=0, keepdims=True)
        idx = jnp.min(jnp.where(s == m, iota, n - 1), axis=0, keepdims=True)
        vals.append(m)
        idxs.append(idx)
        s = jnp.where(iota == idx, -jnp.inf, s)
    return jnp.concatenate(vals, axis=0), jnp.concatenate(idxs, axis=0)


def _pick_rows(sel, table):
    out = jnp.zeros(sel.shape, table.dtype)
    for a in range(table.shape[0]):
        out = jnp.where(sel == a, table[a:a + 1], out)
    return out


def _peer_select_body(x_ref, g_ref, sh_ref, sc_ref, wqh_ref, wql_ref, kh_ref, kl_ref,
                      h_ref, eid_ref, gate_ref):
    h = _adaln(x_ref[0], g_ref[...], sh_ref[0], sc_ref[0])
    h_ref[0] = h
    hh, hl = _split_bf16(h)
    q = _dot(hh, wqh_ref[...]) + _dot(hh, wql_ref[...]) + _dot(hl, wqh_ref[...])
    eids, gates = [], []
    for hd in range(PEER_HEADS):
        sv, si = [], []
        for s in range(2):
            c0 = hd * PEER_DKEY + s * PEER_HALF
            qh, ql = _split_bf16(q[:, c0:c0 + PEER_HALF])
            kh = kh_ref[s, hd]
            st = _dot_nt(kh, qh) + _dot_nt(kh, ql) + _dot_nt(kl_ref[s, hd], qh)
            v, i = _topk_rows(st, PEER_TOPK)
            sv.append(v)
            si.append(i)
        comb = jnp.concatenate([sv[0][a:a + 1] + sv[1] for a in range(PEER_TOPK)], axis=0)
        cv, ci = _topk_rows(comb, PEER_TOPK)
        shift = int(math.log2(PEER_TOPK))
        i1 = _pick_rows(ci >> shift, si[0])
        i2 = _pick_rows(ci & (PEER_TOPK - 1), si[1])
        eids.append(i1 * N_KEYS + i2)
        e = jnp.exp(cv - cv[0:1])
        gates.append(e / jnp.sum(e, axis=0, keepdims=True))
    eid_ref[0] = jnp.concatenate(eids, axis=0).astype(F32).T.astype(I32)
    gate_ref[0] = jnp.concatenate(gates, axis=0).T


def _peer_select(x, g, shift, scale, wq_hi, wq_lo, k_hi, k_lo):
    b, l, d = x.shape
    tl = min(l, 256)
    bm = shift.shape[0]
    tok = lambda n: pl.BlockSpec((1, tl, n), lambda i, j: (i, j, 0))
    return pl.pallas_call(
        _peer_select_body,
        grid=(b, l // tl),
        in_specs=[tok(d), _const_spec((1, d)), _mod_spec(bm), _mod_spec(bm),
                  _const_spec(wq_hi.shape), _const_spec(wq_lo.shape),
                  _const_spec(k_hi.shape), _const_spec(k_lo.shape)],
        out_specs=[tok(d), tok(PEER_SLOTS), tok(PEER_SLOTS)],
        out_shape=[jax.ShapeDtypeStruct((b, l, d), F32),
                   jax.ShapeDtypeStruct((b, l, PEER_SLOTS), I32),
                   jax.ShapeDtypeStruct((b, l, PEER_SLOTS), F32)],
        compiler_params=_cparams(("arbitrary", "arbitrary")),
        name="peer_select",
    )(x, g.reshape(1, d), shift, scale, wq_hi, wq_lo, k_hi, k_lo)


def _pack_body(t_ref, o_ref):
    t = t_ref[...]
    half = t.shape[1] // 2
    lo = lax.bitcast_convert_type(t[:, :half].astype(BF16).astype(F32), jnp.uint32)
    hi = lax.bitcast_convert_type(t[:, half:].astype(BF16).astype(F32), jnp.uint32)
    word = (hi & jnp.uint32(0xFFFF0000)) | (lo >> 16)
    o_ref[...] = lax.bitcast_convert_type(word, I32)


def _pack_table(tab):
    e, d = tab.shape
    r = 512
    out = pl.pallas_call(
        _pack_body,
        grid=(e // r,),
        in_specs=[pl.BlockSpec((r, d), lambda i: (i, 0))],
        out_specs=pl.BlockSpec((r, d // 2), lambda i: (i, 0)),
        out_shape=jax.ShapeDtypeStruct((e, d // 2), I32),
        compiler_params=_cparams(("arbitrary",)),
        name="pack_table",
    )(tab)
    return out.reshape(e, HALF_ROWS, LANES)


def _unpack_words(w):
    lo = lax.bitcast_convert_type(w << 16, F32)
    hi = lax.bitcast_convert_type(w & jnp.int32(-65536), F32)
    return lo, hi


def _peer_u_body(eid_ref, hr_ref, gate_ref, tab_ref, act_ref, ps_ref, araw_ref, *, tb):
    ones = jnp.ones((SUBLANES, LANES), BF16)
    rr = lax.broadcasted_iota(I32, (PEER_SLOTS * HALF_ROWS, PEER_SLOTS), 0)
    cc = lax.broadcasted_iota(I32, (PEER_SLOTS * HALF_ROWS, PEER_SLOTS), 1)
    fold = jnp.where((rr >> 2) == cc, 1.0, 0.0).astype(BF16)

    def token(t, carry):
        xr = hr_ref[0, pl.ds(pl.multiple_of(t * ROW_VREGS, ROW_VREGS), ROW_VREGS), :]
        x_lo = xr[0:HALF_ROWS]
        x_hi = xr[HALF_ROWS:ROW_VREGS]
        for j in range(PEER_SLOTS):
            lo, hi = _unpack_words(tab_ref[eid_ref[0, t, j]])
            ps_ref[j * HALF_ROWS:(j + 1) * HALF_ROWS, :] = lo * x_lo + hi * x_hi
        lane_sums = _dot_nt(ones, ps_ref[...].astype(BF16))
        rh, rl = _split_bf16(lane_sums)
        a = _dot(rh, fold) + _dot(rl, fold)
        araw_ref[pl.ds(t, 1), :] = a[0:1]
        return carry

    lax.fori_loop(0, tb, token, 0)
    a = araw_ref[...]
    act_ref[0] = 0.5 * a * (1.0 + lax.erf(a * (2.0 ** -0.5))) * gate_ref[0]


def _peer_v_body(eid_ref, act_ref, xr_ref, gate_ref, tab_ref, o_ref, wb_ref, *, tb):
    ii = lax.broadcasted_iota(I32, (PEER_SLOTS, PEER_SLOTS), 0)
    jj = lax.broadcasted_iota(I32, (PEER_SLOTS, PEER_SLOTS), 1)
    eye = ii == jj
    ones = jnp.ones((PEER_SLOTS, LANES), BF16)
    gate = gate_ref[0]

    def token(t, carry):
        a = act_ref[0, pl.ds(t, 1), :]
        diag = jnp.where(eye, jnp.broadcast_to(a, (PEER_SLOTS, PEER_SLOTS)), 0.0)
        wb_ref[...] = _dot(diag.astype(BF16), ones)
        acc_lo = jnp.zeros((HALF_ROWS, LANES), F32)
        acc_hi = jnp.zeros((HALF_ROWS, LANES), F32)
        for j in range(PEER_SLOTS):
            lo, hi = _unpack_words(tab_ref[eid_ref[0, t, j]])
            wj = wb_ref[j:j + 1, :]
            acc_lo = acc_lo + lo * wj
            acc_hi = acc_hi + hi * wj
        rows = pl.ds(pl.multiple_of(t * ROW_VREGS, ROW_VREGS), ROW_VREGS)
        o_ref[0, rows, :] = xr_ref[0, rows, :] + gate * jnp.concatenate([acc_lo, acc_hi], axis=0)
        return carry

    lax.fori_loop(0, tb, token, 0)


def _peer_retrieve(x, h, eid, gate_w, mod_gate, tab_u, tab_v):
    b, l, d = x.shape
    tb = 128
    bm = mod_gate.shape[0]
    xr = x.reshape(b, l * ROW_VREGS, LANES)
    hr = h.reshape(b, l * ROW_VREGS, LANES)
    gate_rows = mod_gate.reshape(bm, ROW_VREGS, LANES)
    tok = pl.BlockSpec((1, tb, PEER_SLOTS), lambda i, j: (i, j, 0))
    eid_spec = pl.BlockSpec((1, tb, PEER_SLOTS), lambda i, j: (i, j, 0), memory_space=pltpu.SMEM)
    row_spec = pl.BlockSpec((1, tb * ROW_VREGS, LANES), lambda i, j: (i, j, 0))
    tab_spec = pl.BlockSpec(memory_space=pltpu.VMEM)
    if bm == 1:
        gate_spec = pl.BlockSpec((1, ROW_VREGS, LANES), lambda i, j: (0, 0, 0))
    else:
        gate_spec = pl.BlockSpec((1, ROW_VREGS, LANES), lambda i, j: (i, 0, 0))
    act = pl.pallas_call(
        functools.partial(_peer_u_body, tb=tb),
        grid=(b, l // tb),
        in_specs=[eid_spec, row_spec, tok, tab_spec],
        out_specs=tok,
        out_shape=jax.ShapeDtypeStruct((b, l, PEER_SLOTS), F32),
        scratch_shapes=[pltpu.VMEM((PEER_SLOTS * HALF_ROWS, LANES), F32),
                        pltpu.VMEM((tb, PEER_SLOTS), F32)],
        compiler_params=_cparams(("arbitrary", "arbitrary")),
        name="peer_u",
    )(eid, hr, gate_w, tab_u)
    out = pl.pallas_call(
        functools.partial(_peer_v_body, tb=tb),
        grid=(b, l // tb),
        in_specs=[eid_spec, tok, row_spec, gate_spec, tab_spec],
        out_specs=row_spec,
        out_shape=jax.ShapeDtypeStruct(xr.shape, F32),
        scratch_shapes=[pltpu.VMEM((PEER_SLOTS, LANES), F32)],
        compiler_params=_cparams(("arbitrary", "arbitrary")),
        name="peer_v",
    )(eid, act, xr, gate_rows, tab_v)
    return out.reshape(b, l, d)


def _final_norm_body(x_ref, g_ref, o_ref):
    x = x_ref[0]
    o_ref[0] = x * lax.rsqrt(jnp.mean(x * x, axis=-1, keepdims=True) + EPS) * g_ref[...]


def _final_norm(x, g):
    b, l, d = x.shape
    tl = min(l, 512)
    return pl.pallas_call(
        _final_norm_body,
        grid=(b, l // tl),
        in_specs=[pl.BlockSpec((1, tl, d), lambda i, j: (i, j, 0)), _const_spec((1, d))],
        out_specs=pl.BlockSpec((1, tl, d), lambda i, j: (i, j, 0)),
        out_shape=jax.ShapeDtypeStruct(x.shape, F32),
        compiler_params=_cparams(("arbitrary", "arbitrary")),
        name="final_norm",
    )(x, g.reshape(1, d))


def _group_blocked(cols, per_group, groups):
    d = cols[0].shape[0]
    blocks = []
    for g in range(groups):
        parts = [c[:, g * per_group:(g + 1) * per_group] for c in cols]
        used = per_group * len(cols)
        blocks.append(jnp.concatenate(parts + [jnp.zeros((d, LANES - used), cols[0].dtype)], axis=1))
    return jnp.concatenate(blocks, axis=1)


def _group_blocked_vec(vecs, per_group, groups):
    rows = []
    for g in range(groups):
        parts = [jnp.zeros((per_group,), F32) if v is None else v[g * per_group:(g + 1) * per_group]
                 for v in vecs]
        used = per_group * len(vecs)
        rows.append(jnp.concatenate(parts + [jnp.zeros((LANES - used,), F32)]))
    return jnp.stack(rows).reshape(groups, 1, LANES)


def _ssd_params(w_in, conv_w, conv_b, dt_bias, a_log, d_skip, norm_g, w_out):
    dt0 = SSD_INNER + SSD_CONV_DIM
    w_dt = _group_blocked([w_in[:, dt0:dt0 + SSD_HEADS], w_in[:, dt0 + SSD_HEADS:]], SSD_HPG, SSD_GROUPS)
    return dict(
        ws=[w_in[:, :SSD_INNER].astype(BF16), w_in[:, SSD_INNER:dt0].astype(BF16), w_dt.astype(BF16)],
        conv_w=conv_w, conv_b=conv_b,
        bias=_group_blocked_vec([dt_bias[0], dt_bias[1]], SSD_HPG, SSD_GROUPS),
        alog=_group_blocked_vec([a_log[0], a_log[1]], SSD_HPG, SSD_GROUPS),
        d_skip=jnp.repeat(d_skip, SSD_HEAD_DIM).reshape(1, SSD_INNER),
        norm_g=norm_g.reshape(1, SSD_INNER),
        w_out=w_out.astype(BF16),
    )


def _gdn_params(w_in, conv_w, dt_bias, a_log, norm_g, w_out):
    ab0 = GDN_CONV_DIM + GDN_VW
    h = GDN_HEADS
    ab = [w_in[:, ab0 + i * h:ab0 + (i + 1) * h] for i in range(4)]
    return dict(
        ws=[w_in[:, :GDN_CONV_DIM].astype(BF16), w_in[:, GDN_CONV_DIM:ab0].astype(BF16),
            _group_blocked(ab, GDN_HPG, GDN_GROUPS).astype(BF16)],
        conv_w=conv_w,
        bias=_group_blocked_vec([None, None, dt_bias[0], dt_bias[1]], GDN_HPG, GDN_GROUPS),
        alog=_group_blocked_vec([None, None, a_log[0], a_log[1]], GDN_HPG, GDN_GROUPS),
        norm_g=norm_g.reshape(1, GDN_DV),
        w_out=w_out.astype(BF16),
    )


def _peer_params(w_q, keys, u_tab, v_tab):
    wq_hi, wq_lo = _split_bf16(w_q)
    k_hi, k_lo = _split_bf16(keys)
    return dict(wq_hi=wq_hi, wq_lo=wq_lo, k_hi=k_hi, k_lo=k_lo,
                tab_u=_pack_table(u_tab), tab_v=_pack_table(v_tab))


def _ssd_layer(x, m, g, p, s0f=None, s0b=None):
    z, xbc, dt = _norm_proj(x, g, m[0], m[1], p["ws"])
    xbc = _conv_silu(xbc, p["conv_w"], p["conv_b"])
    yf, yb, sf, sb = _ssd_scan(xbc, dt, p["bias"], p["alog"], s0f, s0b)
    x = _ssd_out(yf, yb, xbc, z, p["d_skip"], p["norm_g"], p["w_out"], x, m[2])
    return x, sf, sb


def _gdn_layer(x, m, g, p, s0f=None, s0b=None):
    qkv, z, ab = _norm_proj(x, g, m[0], m[1], p["ws"])
    qkv = _conv_silu(qkv, p["conv_w"], jnp.zeros((GDN_CONV_DIM,), F32),
                     n_l2=2 * GDN_QK // 512, n_qscale=GDN_QK // 512)
    of, ob, sf, sb = _gdn_scan(qkv, ab, p["bias"], p["alog"], s0f, s0b)
    x = _gdn_out(of, ob, z, p["norm_g"], p["w_out"], x, m[2])
    return x, sf, sb


def _peer_layer(x, m, g, p):
    h, eid, gate_w = _peer_select(x, g, m[3], m[4], p["wq_hi"], p["wq_lo"], p["k_hi"], p["k_lo"])
    return _peer_retrieve(x, h, eid, gate_w, m[5], p["tab_u"], p["tab_v"])


def kernel(x_prompt, x_sample, c, state_ssd_fwd, state_ssd_bwd, state_gdn_fwd, state_gdn_bwd, c_ctx, w_mod, b_mod, norm_mix_g, norm_ffn_g, ssd_w_in, ssd_conv_w, ssd_conv_b, ssd_dt_bias, ssd_a_log, ssd_d, ssd_norm_g, ssd_w_out, gdn_w_in, gdn_conv_w, gdn_dt_bias, gdn_a_log, gdn_norm_g, gdn_w_out, peer_w_q, peer_keys, peer_u, peer_v, final_norm_g):
    d = D_MODEL
    nb = c.shape[0]
    rows = 2 * SUBLANES
    c_all = jnp.zeros((rows, d), F32).at[0].set(c_ctx).at[1:1 + nb].set(c)
    mod = _modulation(c_all, w_mod, b_mod)
    xp = x_prompt
    xs = _add_pos(x_sample, _grid_pos_embed(x_sample.shape[1]))
    ssd_f, ssd_b, gdn_f, gdn_b = [], [], [], []
    for i in range(DEPTH):
        mp = [mod[i, 0:1, k * d:(k + 1) * d].reshape(1, 1, d) for k in range(6)]
        ms = [mod[i, 1:1 + nb, k * d:(k + 1) * d].reshape(nb, 1, d) for k in range(6)]
        j = i // 2
        if i % 2 == 0:
            p = _ssd_params(ssd_w_in[j], ssd_conv_w[j], ssd_conv_b[j], ssd_dt_bias[j], ssd_a_log[j],
                            ssd_d[j], ssd_norm_g[j], ssd_w_out[j])
            xp, sf, sb = _ssd_layer(xp, mp, norm_mix_g[i], p)
            xs, _, _ = _ssd_layer(xs, ms, norm_mix_g[i], p, state_ssd_fwd[:, j], state_ssd_bwd[:, j])
            ssd_f.append(sf)
            ssd_b.append(sb)
        else:
            p = _gdn_params(gdn_w_in[j], gdn_conv_w[j], gdn_dt_bias[j], gdn_a_log[j], gdn_norm_g[j], gdn_w_out[j])
            xp, sf, sb = _gdn_layer(xp, mp, norm_mix_g[i], p)
            xs, _, _ = _gdn_layer(xs, ms, norm_mix_g[i], p, state_gdn_fwd[:, j], state_gdn_bwd[:, j])
            gdn_f.append(sf)
            gdn_b.append(sb)
        pp = _peer_params(peer_w_q[i], peer_keys[i], peer_u[i], peer_v[i])
        xp = _peer_layer(xp, mp, norm_ffn_g[i], pp)
        xs = _peer_layer(xs, ms, norm_ffn_g[i], pp)
    y_prompt = _final_norm(xp, final_norm_g)
    y_sample = _final_norm(xs, final_norm_g)
    return (y_prompt, y_sample, jnp.stack(ssd_f, axis=1), jnp.stack(ssd_b, axis=1),
            jnp.stack(gdn_f, axis=1), jnp.stack(gdn_b, axis=1))
```

```python
import functools
import math

import jax
import jax.numpy as jnp
from jax import lax
from jax.experimental import pallas as pl
from jax.experimental.pallas import tpu as pltpu

F32 = jnp.float32
BF16 = jnp.bfloat16
I32 = jnp.int32
HIGHEST = lax.Precision.HIGHEST

D_MODEL = 1024
DEPTH = 2
GRID_W = 64
CHUNK = 64
EPS = 1e-6
SSD_INNER = 2 * D_MODEL
SSD_HEAD_DIM = 64
SSD_HEADS = SSD_INNER // SSD_HEAD_DIM
SSD_GROUPS = 4
SSD_HPG = SSD_HEADS // SSD_GROUPS
SSD_STATE = 128
SSD_GN = SSD_GROUPS * SSD_STATE
SSD_CONV_DIM = SSD_INNER + 2 * SSD_GN
GDN_HEADS = 8
GDN_DK = 128
GDN_DV = 256
GDN_QK = GDN_HEADS * GDN_DK
GDN_VW = GDN_HEADS * GDN_DV
GDN_CONV_DIM = 2 * GDN_QK + GDN_VW
GDN_HPG = 4
GDN_GROUPS = GDN_HEADS // GDN_HPG
PEER_HEADS = 8
N_KEYS = 128
N_EXPERTS = N_KEYS * N_KEYS
PEER_TOPK = 16
PEER_DKEY = 256
PEER_HALF = PEER_DKEY // 2
PEER_SLOTS = PEER_HEADS * PEER_TOPK

LANES = 128
SUBLANES = 8
ROW_VREGS = D_MODEL // LANES
HALF_ROWS = ROW_VREGS // 2
VMEM_LIMIT = 56 * 1024 * 1024


def _cparams(sem):
    return pltpu.CompilerParams(dimension_semantics=sem, vmem_limit_bytes=VMEM_LIMIT)


def _const_spec(shape):
    nd = len(shape)
    return pl.BlockSpec(shape, lambda *_: (0,) * nd, pipeline_mode=pl.Buffered(1))


def _mod_spec(bm):
    if bm == 1:
        return pl.BlockSpec((1, 1, D_MODEL), lambda b, *_: (0, 0, 0))
    return pl.BlockSpec((1, 1, D_MODEL), lambda b, *_: (b, 0, 0))


def _silu(x):
    return x * jax.nn.sigmoid(x)


def _split_bf16(a):
    hi = a.astype(BF16)
    lo = (a - hi.astype(F32)).astype(BF16)
    return hi, lo


def _dot(a, b):
    return jnp.dot(a, b, preferred_element_type=F32)


def _dot_nt(a, b):
    return lax.dot_general(a, b, (((1,), (1,)), ((), ())), preferred_element_type=F32)


def _adaln(x, g, shift, scale):
    ms = jnp.mean(x * x, axis=-1, keepdims=True)
    return x * lax.rsqrt(ms + EPS) * g * (1.0 + scale) + shift


def _mod_body(c_ref, w_ref, b_ref, o_ref):
    s = _silu(c_ref[...])
    o_ref[0] = jnp.dot(s, w_ref[0], precision=HIGHEST, preferred_element_type=F32) + b_ref[0]


def _modulation(c_all, w_mod, b_mod):
    depth, d, n = w_mod.shape
    rows = c_all.shape[0]
    tn = 1536
    return pl.pallas_call(
        _mod_body,
        grid=(depth, n // tn),
        in_specs=[
            pl.BlockSpec((rows, d), lambda i, j: (0, 0)),
            pl.BlockSpec((1, d, tn), lambda i, j: (i, 0, j)),
            pl.BlockSpec((1, 1, tn), lambda i, j: (i, 0, j)),
        ],
        out_specs=pl.BlockSpec((1, rows, tn), lambda i, j: (i, 0, j)),
        out_shape=jax.ShapeDtypeStruct((depth, rows, n), F32),
        compiler_params=_cparams(("arbitrary", "arbitrary")),
        name="modulation",
    )(c_all, w_mod, b_mod.reshape(depth, 1, n))


def _grid_pos_embed(n):
    rows = n // GRID_W
    r = jnp.repeat(jnp.arange(rows, dtype=F32), GRID_W)
    col = jnp.tile(jnp.arange(GRID_W, dtype=F32), rows)
    quarter = D_MODEL // 4
    omega = 1.0 / (10000.0 ** (jnp.arange(quarter, dtype=F32) / quarter))

    def enc(p):
        ang = p[:, None] * omega[None, :]
        return jnp.concatenate([jnp.sin(ang), jnp.cos(ang)], axis=-1)

    return jnp.concatenate([enc(r), enc(col)], axis=-1)


def _add_body(x_ref, p_ref, o_ref):
    o_ref[0] = x_ref[0] + p_ref[...]


def _add_pos(x, pe):
    b, l, d = x.shape
    tl = 512
    return pl.pallas_call(
        _add_body,
        grid=(l // tl, b),
        in_specs=[pl.BlockSpec((1, tl, d), lambda i, j: (j, i, 0)),
                  pl.BlockSpec((tl, d), lambda i, j: (i, 0))],
        out_specs=pl.BlockSpec((1, tl, d), lambda i, j: (j, i, 0)),
        out_shape=jax.ShapeDtypeStruct(x.shape, F32),
        compiler_params=_cparams(("arbitrary", "arbitrary")),
        name="add_pos",
    )(x, pe)


def _norm_proj_body(x_ref, g_ref, sh_ref, sc_ref, *refs, n_w):
    h = _adaln(x_ref[0], g_ref[...], sh_ref[0], sc_ref[0]).astype(BF16)
    for w_ref, o_ref in zip(refs[:n_w], refs[n_w:]):
        o_ref[0] = _dot(h, w_ref[...])


def _norm_proj(x, g, shift, scale, ws):
    b, l, d = x.shape
    tl = min(l, 256)
    bm = shift.shape[0]
    in_specs = [
        pl.BlockSpec((1, tl, d), lambda i, j: (i, j, 0)),
        _const_spec((1, d)),
        _mod_spec(bm),
        _mod_spec(bm),
    ] + [_const_spec(w.shape) for w in ws]
    return pl.pallas_call(
        functools.partial(_norm_proj_body, n_w=len(ws)),
        grid=(b, l // tl),
        in_specs=in_specs,
        out_specs=[pl.BlockSpec((1, tl, w.shape[1]), lambda i, j: (i, j, 0)) for w in ws],
        out_shape=[jax.ShapeDtypeStruct((b, l, w.shape[1]), F32) for w in ws],
        compiler_params=_cparams(("arbitrary", "arbitrary")),
        name="norm_proj",
    )(x, g.reshape(1, d), shift, scale, *ws)


def _conv_body(x_ref, xm_ref, xp_ref, w_ref, b_ref, o_ref, *, n_l2, n_qscale, hd):
    i = pl.program_id(1)
    j = pl.program_id(2)
    x = x_ref[0]
    r, tc = x.shape
    row = lax.broadcasted_iota(I32, (r, tc), 0)
    prev_row = jnp.where(i == 0, 0.0, xm_ref[0][SUBLANES - 1:SUBLANES, :])
    next_row = jnp.where(i == pl.num_programs(1) - 1, 0.0, xp_ref[0][0:1, :])
    x_prev = jnp.where(row == 0, prev_row, pltpu.roll(x, 1, axis=0))
    x_next = jnp.where(row == r - 1, next_row, pltpu.roll(x, r - 1, axis=0))
    w = w_ref[...]
    y = _silu(x_prev * w[0:1] + x * w[1:2] + x_next * w[2:3] + b_ref[...])
    if n_l2 == 0:
        o_ref[0] = y
        return

    @pl.when(j >= n_l2)
    def _():
        o_ref[0] = y

    @pl.when(j < n_l2)
    def _():
        qs = jnp.where(j < n_qscale, hd ** -0.5, 1.0)
        for k in range(tc // hd):
            yk = y[:, k * hd:(k + 1) * hd]
            ss = jnp.sum(yk * yk, axis=-1, keepdims=True)
            o_ref[0, :, k * hd:(k + 1) * hd] = yk * (lax.rsqrt(ss + EPS) * qs)


def _conv_silu(x, w, bias, n_l2=0, n_qscale=0, hd=GDN_DK):
    b, l, c = x.shape
    r = min(l, 512)
    tc = 512
    rs = r // SUBLANES
    last = l // SUBLANES - 1
    return pl.pallas_call(
        functools.partial(_conv_body, n_l2=n_l2, n_qscale=n_qscale, hd=hd),
        grid=(b, l // r, c // tc),
        in_specs=[
            pl.BlockSpec((1, r, tc), lambda bi, i, j: (bi, i, j)),
            pl.BlockSpec((1, SUBLANES, tc), lambda bi, i, j: (bi, jnp.maximum(i * rs - 1, 0), j)),
            pl.BlockSpec((1, SUBLANES, tc), lambda bi, i, j: (bi, jnp.minimum((i + 1) * rs, last), j)),
            pl.BlockSpec((3, tc), lambda bi, i, j: (0, j)),
            pl.BlockSpec((1, tc), lambda bi, i, j: (0, j)),
        ],
        out_specs=pl.BlockSpec((1, r, tc), lambda bi, i, j: (bi, i, j)),
        out_shape=jax.ShapeDtypeStruct(x.shape, F32),
        compiler_params=_cparams(("arbitrary", "arbitrary", "arbitrary")),
        name="conv_silu",
    )(x, x, x, w, bias.reshape(1, c))


def _chunk_masks(fwd):
    ii = lax.broadcasted_iota(I32, (CHUNK, CHUNK), 0)
    jj = lax.broadcasted_iota(I32, (CHUNK, CHUNK), 1)
    incl = (ii >= jj) if fwd else (ii <= jj)
    strict = (ii > jj) if fwd else (ii < jj)
    return incl, strict


def _cumsum_chunk(incl, v):
    return jnp.dot(incl.astype(F32), v, precision=HIGHEST, preferred_element_type=F32)


def _softplus(x):
    return jnp.maximum(x, 0.0) + jnp.log1p(jnp.exp(-jnp.abs(x)))


def _ssd_chunk(dirs, bias, a):
    p = SSD_HEAD_DIM
    units = []
    for x_ref, b_ref, c_ref, dt_ref, states, y_ref, col0, fwd in dirs:
        incl, _ = _chunk_masks(fwd)
        x = x_ref[0]
        bm_t = b_ref[0].T.astype(BF16)
        cm = c_ref[0].astype(BF16)
        dt = _softplus(dt_ref[0] + bias)
        acs = _cumsum_chunk(incl, dt * a)
        acs_t = acs.T
        tot = acs[CHUNK - 1:CHUNK] if fwd else acs[0:1]
        e_acs = jnp.exp(acs)
        e_rest = jnp.exp(tot - acs)
        dec = jnp.exp(tot)
        cb = _dot(cm, bm_t)
        for k in range(SSD_HPG):
            cc = col0 + k
            lm = jnp.exp(jnp.where(incl, acs[:, cc:cc + 1] - acs_t[cc:cc + 1, :], -jnp.inf))
            units.append(dict(
                bm_t=bm_t, cm=cm, m=(cb * lm).astype(BF16), xdt=x[:, k * p:(k + 1) * p] * dt[:, cc:cc + 1],
                e_acs=e_acs[:, cc:cc + 1], e_rest=e_rest[:, cc:cc + 1], dec=dec[:, cc:cc + 1],
                s_ref=states[k], y_ref=y_ref, cols=slice(k * p, (k + 1) * p)))
    ss = [u["s_ref"][...] for u in units]
    y_in = [_dot(u["m"], u["xdt"].astype(BF16)) for u in units]
    y_x = [_dot(u["cm"], s.astype(BF16)) * u["e_acs"] for u, s in zip(units, ss)]
    for u, yi, yx in zip(units, y_in, y_x):
        u["y_ref"][0, :, u["cols"]] = yi + yx
    new = [s * u["dec"] + _dot(u["bm_t"], (u["xdt"] * u["e_rest"]).astype(BF16)) for u, s in zip(units, ss)]
    for u, s in zip(units, new):
        u["s_ref"][...] = s


def _load_states(states, s0_ref, transpose=False):
    for k, s in enumerate(states):
        if s0_ref is None:
            s[...] = jnp.zeros_like(s)
        else:
            s[...] = s0_ref[0, k].T if transpose else s0_ref[0, k]


def _store_states(out_ref, states, transpose=False):
    for k, s in enumerate(states):
        out_ref[0, k] = s[...].T if transpose else s[...]


def _ssd_body(*refs, has_state):
    (xf, bf, cf, dtf, xb, bb, cbk, dtb, bias_ref, alog_ref), refs = refs[:10], refs[10:]
    s0f = s0b = None
    if has_state:
        (s0f, s0b), refs = refs[:2], refs[2:]
    (yf, yb, sfo, sbo), states = refs[:4], refs[4:]
    sf, sb = states[:SSD_HPG], states[SSD_HPG:]
    c = pl.program_id(2)

    @pl.when(c == 0)
    def _():
        _load_states(sf, s0f, transpose=True)
        _load_states(sb, s0b, transpose=True)

    bias = bias_ref[0]
    a = -jnp.exp(alog_ref[0])
    _ssd_chunk([(xf, bf, cf, dtf, sf, yf, 0, True), (xb, bb, cbk, dtb, sb, yb, SSD_HPG, False)], bias, a)

    @pl.when(c == pl.num_programs(2) - 1)
    def _():
        _store_states(sfo, sf, transpose=True)
        _store_states(sbo, sb, transpose=True)


def _ssd_scan(xbc, dt, bias, alog, s0f=None, s0b=None):
    b, l, _ = xbc.shape
    nc = l // CHUNK
    gw = SSD_HPG * SSD_HEAD_DIM
    boff = SSD_INNER // SSD_STATE
    coff = boff + SSD_GROUPS
    has_state = s0f is not None

    def seq_specs(ci):
        return [
            pl.BlockSpec((1, CHUNK, gw), lambda bi, g, c: (bi, ci(c), g)),
            pl.BlockSpec((1, CHUNK, SSD_STATE), lambda bi, g, c: (bi, ci(c), boff + g)),
            pl.BlockSpec((1, CHUNK, SSD_STATE), lambda bi, g, c: (bi, ci(c), coff + g)),
            pl.BlockSpec((1, CHUNK, LANES), lambda bi, g, c: (bi, ci(c), g)),
        ]

    fw = lambda c: c
    bw = lambda c: nc - 1 - c
    st_spec = pl.BlockSpec((1, SSD_HPG, SSD_HEAD_DIM, SSD_STATE), lambda bi, g, c: (bi, g, 0, 0))
    par_spec = pl.BlockSpec((1, 1, LANES), lambda bi, g, c: (g, 0, 0))
    in_specs = seq_specs(fw) + seq_specs(bw) + [par_spec, par_spec]
    args = [xbc, xbc, xbc, dt, xbc, xbc, xbc, dt, bias, alog]
    if has_state:
        in_specs += [st_spec, st_spec]
        args += [s0f, s0b]
    st_shape = jax.ShapeDtypeStruct((b, SSD_HEADS, SSD_HEAD_DIM, SSD_STATE), F32)
    y_shape = jax.ShapeDtypeStruct((b, l, SSD_INNER), F32)
    return pl.pallas_call(
        functools.partial(_ssd_body, has_state=has_state),
        grid=(b, SSD_GROUPS, nc),
        in_specs=in_specs,
        out_specs=[
            pl.BlockSpec((1, CHUNK, gw), lambda bi, g, c: (bi, c, g)),
            pl.BlockSpec((1, CHUNK, gw), lambda bi, g, c: (bi, nc - 1 - c, g)),
            st_spec, st_spec,
        ],
        out_shape=[y_shape, y_shape, st_shape, st_shape],
        scratch_shapes=[pltpu.VMEM((SSD_STATE, SSD_HEAD_DIM), F32)] * (2 * SSD_HPG),
        compiler_params=_cparams(("arbitrary", "arbitrary", "arbitrary")),
        name="ssd_scan",
    )(*args)


def _ssd_out_body(yf_ref, yb_ref, xs_ref, z_ref, d_ref, ng_ref, w_ref, x_ref, gate_ref, o_ref):
    y = yf_ref[0] + yb_ref[0] + xs_ref[0] * d_ref[...]
    y = y * _silu(z_ref[0])
    y = y * lax.rsqrt(jnp.mean(y * y, axis=-1, keepdims=True) + EPS) * ng_ref[...]
    o_ref[0] = x_ref[0] + gate_ref[0] * _dot(y.astype(BF16), w_ref[...])


def _ssd_out(yf, yb, xbc, z, d_skip, norm_g, w_out, x, gate):
    b, l, d = x.shape
    tl = min(l, 256)
    inner = SSD_INNER
    tok = lambda n: pl.BlockSpec((1, tl, n), lambda i, j: (i, j, 0))
    return pl.pallas_call(
        _ssd_out_body,
        grid=(b, l // tl),
        in_specs=[tok(inner), tok(inner), tok(inner), tok(inner),
                  _const_spec((1, inner)), _const_spec((1, inner)), _const_spec(w_out.shape),
                  tok(d), _mod_spec(gate.shape[0])],
        out_specs=tok(d),
        out_shape=jax.ShapeDtypeStruct(x.shape, F32),
        compiler_params=_cparams(("arbitrary", "arbitrary")),
        name="ssd_out",
    )(yf, yb, xbc, z, d_skip, norm_g, w_out, x, gate)


def _mm(a, b):
    return _dot(a.astype(BF16), b.astype(BF16))


def _mm_nt(a, b):
    return _dot_nt(a.astype(BF16), b.astype(BF16))


INV_BASE_LOG2 = 3


def _unit_tri_inverses(ms):
    ii = lax.broadcasted_iota(I32, (CHUNK, CHUNK), 0)
    jj = lax.broadcasted_iota(I32, (CHUNK, CHUNK), 1)
    eye = jnp.where(ii == jj, 1.0, 0.0)
    base = (ii >> INV_BASE_LOG2) == (jj >> INV_BASE_LOG2)
    pws = [jnp.where(base, m, 0.0) for m in ms]
    xs = [eye - d for d in pws]
    for _ in range(INV_BASE_LOG2 - 1):
        pws = [_mm(pw, pw) for pw in pws]
        xs = [x + _mm(x, pw) for x, pw in zip(xs, pws)]
    for lg in range(INV_BASE_LOG2, int(math.log2(CHUNK))):
        off_diag = ((ii >> (lg + 1)) == (jj >> (lg + 1))) & ((ii >> lg) != (jj >> lg))
        cxs = [_mm(jnp.where(off_diag, m, 0.0), x) for m, x in zip(ms, xs)]
        xs = [x - _mm(x, cx) for x, cx in zip(xs, cxs)]
    return xs


def _gdn_chunk(dirs, bias, neg_a):
    dk, dv = GDN_DK, GDN_DV
    units = []
    for q_ref, k_ref, v_ref, ab_ref, states, o_ref, off, fwd in dirs:
        incl, strict = _chunk_masks(fwd)
        q = q_ref[0]
        k = k_ref[0]
        v = v_ref[0]
        ab = ab_ref[0]
        beta_all = jax.nn.sigmoid(ab)
        gc = _cumsum_chunk(incl, neg_a * _softplus(ab + bias))
        gc_t = gc.T
        tot = gc[CHUNK - 1:CHUNK] if fwd else gc[0:1]
        e_gc = jnp.exp(gc)
        e_rest = jnp.exp(tot - gc)
        g_last = jnp.exp(tot)
        for h in range(GDN_HPG):
            cb = off + h
            ca = 2 * GDN_HPG + off + h
            kh = k[:, h * dk:(h + 1) * dk]
            beta = beta_all[:, cb:cb + 1]
            units.append(dict(
                incl=incl, strict=strict, kh=kh, kb=kh * beta,
                qh=q[:, h * dk:(h + 1) * dk], vb=v[:, h * dv:(h + 1) * dv] * beta,
                decay=jnp.exp(jnp.where(incl, gc[:, ca:ca + 1] - gc_t[ca:ca + 1, :], -jnp.inf)),
                e_gc=e_gc[:, ca:ca + 1], e_rest=e_rest[:, ca:ca + 1], g_last=g_last[:, ca:ca + 1],
                s_ref=states[h], o_ref=o_ref, cols=slice(h * dv, (h + 1) * dv)))
    ms = [jnp.where(u["strict"], _mm_nt(u["kb"], u["kh"]) * u["decay"], 0.0) for u in units]
    ts = _unit_tri_inverses(ms)
    us = [_mm(t, u["vb"]) for t, u in zip(ts, units)]
    ws = [_mm(t, u["kb"] * u["e_gc"]) for t, u in zip(ts, units)]
    aqk = [jnp.where(u["incl"], _mm_nt(u["qh"], u["kh"]) * u["decay"], 0.0) for u in units]
    ss = [u["s_ref"][...] for u in units]
    vn = [x - _mm(w, s) for x, w, s in zip(us, ws, ss)]
    os_ = [_mm(u["qh"] * u["e_gc"], s) + _mm(a, v) for u, s, a, v in zip(units, ss, aqk, vn)]
    for u, o in zip(units, os_):
        u["o_ref"][0, :, u["cols"]] = o
    new = [s * u["g_last"] + _mm((u["kh"] * u["e_rest"]).T, v) for u, s, v in zip(units, ss, vn)]
    for u, s in zip(units, new):
        u["s_ref"][...] = s


def _gdn_body(*refs, has_state):
    (qf, kf, vf, abf, qb, kb, vb, abb, bias_ref, alog_ref), refs = refs[:10], refs[10:]
    s0f = s0b = None
    if has_state:
        (s0f, s0b), refs = refs[:2], refs[2:]
    (of, ob, sfo, sbo), states = refs[:4], refs[4:]
    sf, sb = states[:GDN_HPG], states[GDN_HPG:]
    c = pl.program_id(2)

    @pl.when(c == 0)
    def _():
        _load_states(sf, s0f)
        _load_states(sb, s0b)

    bias = bias_ref[0]
    neg_a = -jnp.exp(alog_ref[0])
    _gdn_chunk([(qf, kf, vf, abf, sf, of, 0, True), (qb, kb, vb, abb, sb, ob, GDN_HPG, False)], bias, neg_a)

    @pl.when(c == pl.num_programs(2) - 1)
    def _():
        _store_states(sfo, sf)
        _store_states(sbo, sb)


def _gdn_scan(qkv, ab, bias, alog, s0f=None, s0b=None):
    b, l, _ = qkv.shape
    nc = l // CHUNK
    qw = GDN_HPG * GDN_DK
    vw = GDN_HPG * GDN_DV
    koff = GDN_QK // qw
    voff = 2 * GDN_QK // vw
    has_state = s0f is not None

    def seq_specs(ci):
        return [
            pl.BlockSpec((1, CHUNK, qw), lambda bi, g, c: (bi, ci(c), g)),
            pl.BlockSpec((1, CHUNK, qw), lambda bi, g, c: (bi, ci(c), koff + g)),
            pl.BlockSpec((1, CHUNK, vw), lambda bi, g, c: (bi, ci(c), voff + g)),
            pl.BlockSpec((1, CHUNK, LANES), lambda bi, g, c: (bi, ci(c), g)),
        ]

    fw = lambda c: c
    bw = lambda c: nc - 1 - c
    st_spec = pl.BlockSpec((1, GDN_HPG, GDN_DK, GDN_DV), lambda bi, g, c: (bi, g, 0, 0))
    par_spec = pl.BlockSpec((1, 1, LANES), lambda bi, g, c: (g, 0, 0))
    in_specs = seq_specs(fw) + seq_specs(bw) + [par_spec, par_spec]
    args = [qkv, qkv, qkv, ab, qkv, qkv, qkv, ab, bias, alog]
    if has_state:
        in_specs += [st_spec, st_spec]
        args += [s0f, s0b]
    st_shape = jax.ShapeDtypeStruct((b, GDN_HEADS, GDN_DK, GDN_DV), F32)
    o_shape = jax.ShapeDtypeStruct((b, l, GDN_VW), F32)
    return pl.pallas_call(
        functools.partial(_gdn_body, has_state=has_state),
        grid=(b, GDN_GROUPS, nc),
        in_specs=in_specs,
        out_specs=[
            pl.BlockSpec((1, CHUNK, vw), lambda bi, g, c: (bi, c, g)),
            pl.BlockSpec((1, CHUNK, vw), lambda bi, g, c: (bi, nc - 1 - c, g)),
            st_spec, st_spec,
        ],
        out_shape=[o_shape, o_shape, st_shape, st_shape],
        scratch_shapes=[pltpu.VMEM((GDN_DK, GDN_DV), F32)] * (2 * GDN_HPG),
        compiler_params=_cparams(("arbitrary", "arbitrary", "arbitrary")),
        name="gdn_scan",
    )(*args)


def _gdn_out_body(of_ref, ob_ref, z_ref, ng_ref, w_ref, x_ref, gate_ref, o_ref, y_ref):
    dv = GDN_DV
    for h in range(GDN_HEADS):
        sl = slice(h * dv, (h + 1) * dv)
        o = of_ref[0, :, sl] + ob_ref[0, :, sl]
        o = o * lax.rsqrt(jnp.mean(o * o, axis=-1, keepdims=True) + EPS) * ng_ref[...]
        y_ref[:, sl] = (o * _silu(z_ref[0, :, sl])).astype(BF16)
    o_ref[0] = x_ref[0] + gate_ref[0] * _dot(y_ref[...], w_ref[...])


def _gdn_out(of, ob, z, norm_g, w_out, x, gate):
    b, l, d = x.shape
    tl = min(l, 256)
    tok = lambda n: pl.BlockSpec((1, tl, n), lambda i, j: (i, j, 0))
    return pl.pallas_call(
        _gdn_out_body,
        grid=(b, l // tl),
        in_specs=[tok(GDN_VW), tok(GDN_VW), tok(GDN_VW),
                  _const_spec((1, GDN_DV)), _const_spec(w_out.shape),
                  tok(d), _mod_spec(gate.shape[0])],
        out_specs=tok(d),
        out_shape=jax.ShapeDtypeStruct(x.shape, F32),
        scratch_shapes=[pltpu.VMEM((tl, GDN_VW), BF16)],
        compiler_params=_cparams(("arbitrary", "arbitrary")),
        name="gdn_out",
    )(of, ob, z, norm_g, w_out, x, gate)


def _topk_rows(s, k):
    n = s.shape[0]
    iota = lax.broadcasted_iota(I32, s.shape, 0)
    vals, idxs = [], []
    for _ in range(k):
        m = jnp.max(s, axis=0, keepdims=True)
        idx = jnp.min(jnp.where(s == m, iota, n - 1), axis=0, keepdims=True)
        vals.append(m)
        idxs.append(idx)
        s = jnp.where(iota == idx, -jnp.inf, s)
    return jnp.concatenate(vals, axis=0), jnp.concatenate(idxs, axis=0)


def _pick_rows(sel, table):
    out = jnp.zeros(sel.shape, table.dtype)
    for a in range(table.shape[0]):
        out = jnp.where(sel == a, table[a:a + 1], out)
    return out


def _peer_select_body(x_ref, g_ref, sh_ref, sc_ref, wqh_ref, wql_ref, kh_ref, kl_ref,
                      h_ref, eid_ref, gate_ref):
    h = _adaln(x_ref[0], g_ref[...], sh_ref[0], sc_ref[0])
    h_ref[0] = h
    hh, hl = _split_bf16(h)
    q = _dot(hh, wqh_ref[...]) + _dot(hh, wql_ref[...]) + _dot(hl, wqh_ref[...])
    eids, gates = [], []
    for hd in range(PEER_HEADS):
        sv, si = [], []
        for s in range(2):
            c0 = hd * PEER_DKEY + s * PEER_HALF
            qh, ql = _split_bf16(q[:, c0:c0 + PEER_HALF])
            kh = kh_ref[s, hd]
            st = _dot_nt(kh, qh) + _dot_nt(kh, ql) + _dot_nt(kl_ref[s, hd], qh)
            v, i = _topk_rows(st, PEER_TOPK)
            sv.append(v)
            si.append(i)
        comb = jnp.concatenate([sv[0][a:a + 1] + sv[1] for a in range(PEER_TOPK)], axis=0)
        cv, ci = _topk_rows(comb, PEER_TOPK)
        shift = int(math.log2(PEER_TOPK))
        i1 = _pick_rows(ci >> shift, si[0])
        i2 = _pick_rows(ci & (PEER_TOPK - 1), si[1])
        eids.append(i1 * N_KEYS + i2)
        e = jnp.exp(cv - cv[0:1])
        gates.append(e / jnp.sum(e, axis=0, keepdims=True))
    eid_ref[0] = jnp.concatenate(eids, axis=0).astype(F32).T.astype(I32)
    gate_ref[0] = jnp.concatenate(gates, axis=0).T


def _peer_select(x, g, shift, scale, wq_hi, wq_lo, k_hi, k_lo):
    b, l, d = x.shape
    tl = min(l, 256)
    bm = shift.shape[0]
    tok = lambda n: pl.BlockSpec((1, tl, n), lambda i, j: (i, j, 0))
    return pl.pallas_call(
        _peer_select_body,
        grid=(b, l // tl),
        in_specs=[tok(d), _const_spec((1, d)), _mod_spec(bm), _mod_spec(bm),
                  _const_spec(wq_hi.shape), _const_spec(wq_lo.shape),
                  _const_spec(k_hi.shape), _const_spec(k_lo.shape)],
        out_specs=[tok(d), tok(PEER_SLOTS), tok(PEER_SLOTS)],
        out_shape=[jax.ShapeDtypeStruct((b, l, d), F32),
                   jax.ShapeDtypeStruct((b, l, PEER_SLOTS), I32),
                   jax.ShapeDtypeStruct((b, l, PEER_SLOTS), F32)],
        compiler_params=_cparams(("arbitrary", "arbitrary")),
        name="peer_select",
    )(x, g.reshape(1, d), shift, scale, wq_hi, wq_lo, k_hi, k_lo)


def _pack_body(t_ref, o_ref):
    t = t_ref[...]
    half = t.shape[1] // 2
    lo = lax.bitcast_convert_type(t[:, :half].astype(BF16).astype(F32), jnp.uint32)
    hi = lax.bitcast_convert_type(t[:, half:].astype(BF16).astype(F32), jnp.uint32)
    word = (hi & jnp.uint32(0xFFFF0000)) | (lo >> 16)
    o_ref[...] = lax.bitcast_convert_type(word, I32)


def _pack_table(tab):
    e, d = tab.shape
    r = 512
    out = pl.pallas_call(
        _pack_body,
        grid=(e // r,),
        in_specs=[pl.BlockSpec((r, d), lambda i: (i, 0))],
        out_specs=pl.BlockSpec((r, d // 2), lambda i: (i, 0)),
        out_shape=jax.ShapeDtypeStruct((e, d // 2), I32),
        compiler_params=_cparams(("arbitrary",)),
        name="pack_table",
    )(tab)
    return out.reshape(e, HALF_ROWS, LANES)


def _unpack_words(w):
    lo = lax.bitcast_convert_type(w << 16, F32)
    hi = lax.bitcast_convert_type(w & jnp.int32(-65536), F32)
    return lo, hi


PEER_LAG = 2


def _peer_u_body(eid_ref, hr_ref, gate_ref, fold_ref, tab_ref, act_ref, ps_a, ps_b, araw_ref, *, tb):
    ones = jnp.ones((SUBLANES, LANES), BF16)

    def gather(t, ps_ref):
        xr = hr_ref[0, pl.ds(pl.multiple_of(t * ROW_VREGS, ROW_VREGS), ROW_VREGS), :]
        x_lo = xr[0:HALF_ROWS]
        x_hi = xr[HALF_ROWS:ROW_VREGS]
        for j in range(PEER_SLOTS):
            lo, hi = _unpack_words(tab_ref[eid_ref[0, t, j]])
            ps_ref[j * HALF_ROWS:(j + 1) * HALF_ROWS, :] = lo * x_lo + hi * x_hi

    def reduce(ps_ref, row):
        lane_sums = _dot_nt(ones, ps_ref[...].astype(BF16))
        rh, rl = _split_bf16(lane_sums)
        a = _dot(rh, fold_ref[...]) + _dot(rl, fold_ref[...])
        araw_ref[pl.ds(row, 1), :] = a[0:1]

    ps_a[...] = jnp.zeros_like(ps_a)
    ps_b[...] = jnp.zeros_like(ps_b)

    def pair(i, carry):
        t = i * PEER_LAG
        reduce(ps_a, t + SUBLANES - PEER_LAG)
        reduce(ps_b, t + SUBLANES - PEER_LAG + 1)
        gather(t, ps_a)
        gather(t + 1, ps_b)
        return carry

    lax.fori_loop(0, tb // PEER_LAG, pair, 0)
    reduce(ps_a, tb + SUBLANES - PEER_LAG)
    reduce(ps_b, tb + SUBLANES - PEER_LAG + 1)
    a = araw_ref[SUBLANES:tb + SUBLANES, :]
    act_ref[0] = 0.5 * a * (1.0 + lax.erf(a * (2.0 ** -0.5))) * gate_ref[0]


def _peer_v_body(eid_ref, act_ref, xr_ref, gate_ref, eye_ref, tab_ref, o_ref, wb_a, wb_b, *, tb):
    ones = jnp.ones((PEER_SLOTS, LANES), BF16)
    gate = gate_ref[0]

    def spread(t, wb_ref):
        a = act_ref[0, pl.ds(t, 1), :]
        diag = jnp.where(eye_ref[...] != 0.0, a, 0.0)
        wb_ref[...] = _dot(diag.astype(BF16), ones)

    def combine(t, wb_ref):
        n_acc = 2
        acc_lo = [jnp.zeros((HALF_ROWS, LANES), F32) for _ in range(n_acc)]
        acc_hi = [jnp.zeros((HALF_ROWS, LANES), F32) for _ in range(n_acc)]
        for j in range(PEER_SLOTS):
            lo, hi = _unpack_words(tab_ref[eid_ref[0, t, j]])
            wj = wb_ref[j:j + 1, :]
            acc_lo[j % n_acc] = acc_lo[j % n_acc] + lo * wj
            acc_hi[j % n_acc] = acc_hi[j % n_acc] + hi * wj
        rows = pl.ds(pl.multiple_of(t * ROW_VREGS, ROW_VREGS), ROW_VREGS)
        out = jnp.concatenate([acc_lo[0] + acc_lo[1], acc_hi[0] + acc_hi[1]], axis=0)
        o_ref[0, rows, :] = xr_ref[0, rows, :] + gate * out

    spread(0, wb_a)

    def pair(i, carry):
        t = i * PEER_LAG
        spread(t + 1, wb_b)
        combine(t, wb_a)
        spread(jnp.minimum(t + PEER_LAG, tb - 1), wb_a)
        combine(t + 1, wb_b)
        return carry

    lax.fori_loop(0, tb // PEER_LAG, pair, 0)


def _peer_retrieve(x, h, eid, gate_w, mod_gate, tab_u, tab_v):
    b, l, d = x.shape
    tb = 128
    bm = mod_gate.shape[0]
    xr = x.reshape(b, l * ROW_VREGS, LANES)
    hr = h.reshape(b, l * ROW_VREGS, LANES)
    gate_rows = mod_gate.reshape(bm, ROW_VREGS, LANES)
    tok = pl.BlockSpec((1, tb, PEER_SLOTS), lambda i, j: (i, j, 0))
    eid_spec = pl.BlockSpec((1, tb, PEER_SLOTS), lambda i, j: (i, j, 0), memory_space=pltpu.SMEM)
    row_spec = pl.BlockSpec((1, tb * ROW_VREGS, LANES), lambda i, j: (i, j, 0))
    tab_spec = pl.BlockSpec(memory_space=pltpu.VMEM)
    if bm == 1:
        gate_spec = pl.BlockSpec((1, ROW_VREGS, LANES), lambda i, j: (0, 0, 0))
    else:
        gate_spec = pl.BlockSpec((1, ROW_VREGS, LANES), lambda i, j: (i, 0, 0))
    n_part = PEER_SLOTS * HALF_ROWS
    fold = (jnp.arange(n_part)[:, None] // HALF_ROWS == jnp.arange(PEER_SLOTS)[None, :]).astype(BF16)
    eye = jnp.eye(PEER_SLOTS, dtype=F32)
    part = pltpu.VMEM((n_part, LANES), F32)
    act = pl.pallas_call(
        functools.partial(_peer_u_body, tb=tb),
        grid=(b, l // tb),
        in_specs=[eid_spec, row_spec, tok, _const_spec(fold.shape), tab_spec],
        out_specs=tok,
        out_shape=jax.ShapeDtypeStruct((b, l, PEER_SLOTS), F32),
        scratch_shapes=[part, part, pltpu.VMEM((tb + SUBLANES, PEER_SLOTS), F32)],
        compiler_params=_cparams(("arbitrary", "arbitrary")),
        name="peer_u",
    )(eid, hr, gate_w, fold, tab_u)
    spread = pltpu.VMEM((PEER_SLOTS, LANES), F32)
    out = pl.pallas_call(
        functools.partial(_peer_v_body, tb=tb),
        grid=(b, l // tb),
        in_specs=[eid_spec, tok, row_spec, gate_spec, _const_spec(eye.shape), tab_spec],
        out_specs=row_spec,
        out_shape=jax.ShapeDtypeStruct(xr.shape, F32),
        scratch_shapes=[spread, spread],
        compiler_params=_cparams(("arbitrary", "arbitrary")),
        name="peer_v",
    )(eid, act, xr, gate_rows, eye, tab_v)
    return out.reshape(b, l, d)


def _final_norm_body(x_ref, g_ref, o_ref):
    x = x_ref[0]
    o_ref[0] = x * lax.rsqrt(jnp.mean(x * x, axis=-1, keepdims=True) + EPS) * g_ref[...]


def _final_norm(x, g):
    b, l, d = x.shape
    tl = min(l, 512)
    return pl.pallas_call(
        _final_norm_body,
        grid=(b, l // tl),
        in_specs=[pl.BlockSpec((1, tl, d), lambda i, j: (i, j, 0)), _const_spec((1, d))],
        out_specs=pl.BlockSpec((1, tl, d), lambda i, j: (i, j, 0)),
        out_shape=jax.ShapeDtypeStruct(x.shape, F32),
        compiler_params=_cparams(("arbitrary", "arbitrary")),
        name="final_norm",
    )(x, g.reshape(1, d))


def _group_blocked(cols, per_group, groups):
    d = cols[0].shape[0]
    blocks = []
    for g in range(groups):
        parts = [c[:, g * per_group:(g + 1) * per_group] for c in cols]
        used = per_group * len(cols)
        blocks.append(jnp.concatenate(parts + [jnp.zeros((d, LANES - used), cols[0].dtype)], axis=1))
    return jnp.concatenate(blocks, axis=1)


def _group_blocked_vec(vecs, per_group, groups):
    rows = []
    for g in range(groups):
        parts = [jnp.zeros((per_group,), F32) if v is None else v[g * per_group:(g + 1) * per_group]
                 for v in vecs]
        used = per_group * len(vecs)
        rows.append(jnp.concatenate(parts + [jnp.zeros((LANES - used,), F32)]))
    return jnp.stack(rows).reshape(groups, 1, LANES)


def _ssd_params(w_in, conv_w, conv_b, dt_bias, a_log, d_skip, norm_g, w_out):
    dt0 = SSD_INNER + SSD_CONV_DIM
    w_dt = _group_blocked([w_in[:, dt0:dt0 + SSD_HEADS], w_in[:, dt0 + SSD_HEADS:]], SSD_HPG, SSD_GROUPS)
    return dict(
        ws=[w_in[:, :SSD_INNER].astype(BF16), w_in[:, SSD_INNER:dt0].astype(BF16), w_dt.astype(BF16)],
        conv_w=conv_w, conv_b=conv_b,
        bias=_group_blocked_vec([dt_bias[0], dt_bias[1]], SSD_HPG, SSD_GROUPS),
        alog=_group_blocked_vec([a_log[0], a_log[1]], SSD_HPG, SSD_GROUPS),
        d_skip=jnp.repeat(d_skip, SSD_HEAD_DIM).reshape(1, SSD_INNER),
        norm_g=norm_g.reshape(1, SSD_INNER),
        w_out=w_out.astype(BF16),
    )


def _gdn_params(w_in, conv_w, dt_bias, a_log, norm_g, w_out):
    ab0 = GDN_CONV_DIM + GDN_VW
    h = GDN_HEADS
    ab = [w_in[:, ab0 + i * h:ab0 + (i + 1) * h] for i in range(4)]
    return dict(
        ws=[w_in[:, :GDN_CONV_DIM].astype(BF16), w_in[:, GDN_CONV_DIM:ab0].astype(BF16),
            _group_blocked(ab, GDN_HPG, GDN_GROUPS).astype(BF16)],
        conv_w=conv_w,
        bias=_group_blocked_vec([None, None, dt_bias[0], dt_bias[1]], GDN_HPG, GDN_GROUPS),
        alog=_group_blocked_vec([None, None, a_log[0], a_log[1]], GDN_HPG, GDN_GROUPS),
        norm_g=norm_g.reshape(1, GDN_DV),
        w_out=w_out.astype(BF16),
    )


def _peer_params(w_q, keys, u_tab, v_tab):
    wq_hi, wq_lo = _split_bf16(w_q)
    k_hi, k_lo = _split_bf16(keys)
    return dict(wq_hi=wq_hi, wq_lo=wq_lo, k_hi=k_hi, k_lo=k_lo,
                tab_u=_pack_table(u_tab), tab_v=_pack_table(v_tab))


def _ssd_layer(x, m, g, p, s0f=None, s0b=None):
    z, xbc, dt = _norm_proj(x, g, m[0], m[1], p["ws"])
    xbc = _conv_silu(xbc, p["conv_w"], p["conv_b"])
    yf, yb, sf, sb = _ssd_scan(xbc, dt, p["bias"], p["alog"], s0f, s0b)
    x = _ssd_out(yf, yb, xbc, z, p["d_skip"], p["norm_g"], p["w_out"], x, m[2])
    return x, sf, sb


def _gdn_layer(x, m, g, p, s0f=None, s0b=None):
    qkv, z, ab = _norm_proj(x, g, m[0], m[1], p["ws"])
    qkv = _conv_silu(qkv, p["conv_w"], jnp.zeros((GDN_CONV_DIM,), F32),
                     n_l2=2 * GDN_QK // 512, n_qscale=GDN_QK // 512)
    of, ob, sf, sb = _gdn_scan(qkv, ab, p["bias"], p["alog"], s0f, s0b)
    x = _gdn_out(of, ob, z, p["norm_g"], p["w_out"], x, m[2])
    return x, sf, sb


def _peer_layer(x, m, g, p):
    h, eid, gate_w = _peer_select(x, g, m[3], m[4], p["wq_hi"], p["wq_lo"], p["k_hi"], p["k_lo"])
    return _peer_retrieve(x, h, eid, gate_w, m[5], p["tab_u"], p["tab_v"])


def kernel(x_prompt, x_sample, c, state_ssd_fwd, state_ssd_bwd, state_gdn_fwd, state_gdn_bwd, c_ctx, w_mod, b_mod, norm_mix_g, norm_ffn_g, ssd_w_in, ssd_conv_w, ssd_conv_b, ssd_dt_bias, ssd_a_log, ssd_d, ssd_norm_g, ssd_w_out, gdn_w_in, gdn_conv_w, gdn_dt_bias, gdn_a_log, gdn_norm_g, gdn_w_out, peer_w_q, peer_keys, peer_u, peer_v, final_norm_g):
    d = D_MODEL
    nb = c.shape[0]
    rows = 2 * SUBLANES
    c_all = jnp.zeros((rows, d), F32).at[0].set(c_ctx).at[1:1 + nb].set(c)
    mod = _modulation(c_all, w_mod, b_mod)
    xp = x_prompt
    xs = _add_pos(x_sample, _grid_pos_embed(x_sample.shape[1]))
    ssd_f, ssd_b, gdn_f, gdn_b = [], [], [], []
    for i in range(DEPTH):
        mp = [mod[i, 0:1, k * d:(k + 1) * d].reshape(1, 1, d) for k in range(6)]
        ms = [mod[i, 1:1 + nb, k * d:(k + 1) * d].reshape(nb, 1, d) for k in range(6)]
        j = i // 2
        if i % 2 == 0:
            p = _ssd_params(ssd_w_in[j], ssd_conv_w[j], ssd_conv_b[j], ssd_dt_bias[j], ssd_a_log[j],
                            ssd_d[j], ssd_norm_g[j], ssd_w_out[j])
            xp, sf, sb = _ssd_layer(xp, mp, norm_mix_g[i], p)
            xs, _, _ = _ssd_layer(xs, ms, norm_mix_g[i], p, state_ssd_fwd[:, j], state_ssd_bwd[:, j])
            ssd_f.append(sf)
            ssd_b.append(sb)
        else:
            p = _gdn_params(gdn_w_in[j], gdn_conv_w[j], gdn_dt_bias[j], gdn_a_log[j], gdn_norm_g[j], gdn_w_out[j])
            xp, sf, sb = _gdn_layer(xp, mp, norm_mix_g[i], p)
            xs, _, _ = _gdn_layer(xs, ms, norm_mix_g[i], p, state_gdn_fwd[:, j], state_gdn_bwd[:, j])
            gdn_f.append(sf)
            gdn_b.append(sb)
        pp = _peer_params(peer_w_q[i], peer_keys[i], peer_u[i], peer_v[i])
        xp = _peer_layer(xp, mp, norm_ffn_g[i], pp)
        xs = _peer_layer(xs, ms, norm_ffn_g[i], pp)
    y_prompt = _final_norm(xp, final_norm_g)
    y_sample = _final_norm(xs, final_norm_g)
    return (y_prompt, y_sample, jnp.stack(ssd_f, axis=1), jnp.stack(ssd_b, axis=1),
            jnp.stack(gdn_f, axis=1), jnp.stack(gdn_b, axis=1))
```

```python
import functools
import math

import jax
import jax.numpy as jnp
from jax import lax
from jax.experimental import pallas as pl
from jax.experimental.pallas import tpu as pltpu

F32 = jnp.float32
BF16 = jnp.bfloat16
I32 = jnp.int32
HIGHEST = lax.Precision.HIGHEST

D_MODEL = 1024
DEPTH = 2
GRID_W = 64
CHUNK = 64
EPS = 1e-6
SSD_INNER = 2 * D_MODEL
SSD_HEAD_DIM = 64
SSD_HEADS = SSD_INNER // SSD_HEAD_DIM
SSD_GROUPS = 4
SSD_HPG = SSD_HEADS // SSD_GROUPS
SSD_STATE = 128
SSD_GN = SSD_GROUPS * SSD_STATE
SSD_CONV_DIM = SSD_INNER + 2 * SSD_GN
GDN_HEADS = 8
GDN_DK = 128
GDN_DV = 256
GDN_QK = GDN_HEADS * GDN_DK
GDN_VW = GDN_HEADS * GDN_DV
GDN_CONV_DIM = 2 * GDN_QK + GDN_VW
GDN_HPG = 4
GDN_GROUPS = GDN_HEADS // GDN_HPG
PEER_HEADS = 8
N_KEYS = 128
N_EXPERTS = N_KEYS * N_KEYS
PEER_TOPK = 16
PEER_DKEY = 256
PEER_HALF = PEER_DKEY // 2
PEER_SLOTS = PEER_HEADS * PEER_TOPK

LANES = 128
SUBLANES = 8
ROW_VREGS = D_MODEL // LANES
HALF_ROWS = ROW_VREGS // 2
VMEM_LIMIT = 56 * 1024 * 1024


def _cparams(sem):
    return pltpu.CompilerParams(dimension_semantics=sem, vmem_limit_bytes=VMEM_LIMIT)


def _const_spec(shape):
    nd = len(shape)
    return pl.BlockSpec(shape, lambda *_: (0,) * nd, pipeline_mode=pl.Buffered(1))


def _mod_spec(bm):
    if bm == 1:
        return pl.BlockSpec((1, 1, D_MODEL), lambda b, *_: (0, 0, 0))
    return pl.BlockSpec((1, 1, D_MODEL), lambda b, *_: (b, 0, 0))


def _silu(x):
    return x * jax.nn.sigmoid(x)


def _split_bf16(a):
    hi = a.astype(BF16)
    lo = (a - hi.astype(F32)).astype(BF16)
    return hi, lo


def _dot(a, b):
    return jnp.dot(a, b, preferred_element_type=F32)


def _dot_nt(a, b):
    return lax.dot_general(a, b, (((1,), (1,)), ((), ())), preferred_element_type=F32)


def _adaln(x, g, shift, scale):
    ms = jnp.mean(x * x, axis=-1, keepdims=True)
    return x * lax.rsqrt(ms + EPS) * g * (1.0 + scale) + shift


def _mod_body(c_ref, w_ref, b_ref, o_ref):
    s = _silu(c_ref[...])
    o_ref[0] = jnp.dot(s, w_ref[0], precision=HIGHEST, preferred_element_type=F32) + b_ref[0]


def _modulation(c_all, w_mod, b_mod):
    depth, d, n = w_mod.shape
    rows = c_all.shape[0]
    tn = 1536
    return pl.pallas_call(
        _mod_body,
        grid=(depth, n // tn),
        in_specs=[
            pl.BlockSpec((rows, d), lambda i, j: (0, 0)),
            pl.BlockSpec((1, d, tn), lambda i, j: (i, 0, j)),
            pl.BlockSpec((1, 1, tn), lambda i, j: (i, 0, j)),
        ],
        out_specs=pl.BlockSpec((1, rows, tn), lambda i, j: (i, 0, j)),
        out_shape=jax.ShapeDtypeStruct((depth, rows, n), F32),
        compiler_params=_cparams(("arbitrary", "arbitrary")),
        name="modulation",
    )(c_all, w_mod, b_mod.reshape(depth, 1, n))


def _grid_pos_embed(n):
    rows = n // GRID_W
    r = jnp.repeat(jnp.arange(rows, dtype=F32), GRID_W)
    col = jnp.tile(jnp.arange(GRID_W, dtype=F32), rows)
    quarter = D_MODEL // 4
    omega = 1.0 / (10000.0 ** (jnp.arange(quarter, dtype=F32) / quarter))

    def enc(p):
        ang = p[:, None] * omega[None, :]
        return jnp.concatenate([jnp.sin(ang), jnp.cos(ang)], axis=-1)

    return jnp.concatenate([enc(r), enc(col)], axis=-1)


def _add_body(x_ref, p_ref, o_ref):
    o_ref[0] = x_ref[0] + p_ref[...]


def _add_pos(x, pe):
    b, l, d = x.shape
    tl = 512
    return pl.pallas_call(
        _add_body,
        grid=(l // tl, b),
        in_specs=[pl.BlockSpec((1, tl, d), lambda i, j: (j, i, 0)),
                  pl.BlockSpec((tl, d), lambda i, j: (i, 0))],
        out_specs=pl.BlockSpec((1, tl, d), lambda i, j: (j, i, 0)),
        out_shape=jax.ShapeDtypeStruct(x.shape, F32),
        compiler_params=_cparams(("arbitrary", "arbitrary")),
        name="add_pos",
    )(x, pe)


def _norm_proj_body(x_ref, g_ref, sh_ref, sc_ref, *refs, n_w):
    h = _adaln(x_ref[0], g_ref[...], sh_ref[0], sc_ref[0]).astype(BF16)
    for w_ref, o_ref in zip(refs[:n_w], refs[n_w:]):
        o_ref[0] = _dot(h, w_ref[...])


def _norm_proj(x, g, shift, scale, ws):
    b, l, d = x.shape
    tl = min(l, 256)
    bm = shift.shape[0]
    in_specs = [
        pl.BlockSpec((1, tl, d), lambda i, j: (i, j, 0)),
        _const_spec((1, d)),
        _mod_spec(bm),
        _mod_spec(bm),
    ] + [_const_spec(w.shape) for w in ws]
    return pl.pallas_call(
        functools.partial(_norm_proj_body, n_w=len(ws)),
        grid=(b, l // tl),
        in_specs=in_specs,
        out_specs=[pl.BlockSpec((1, tl, w.shape[1]), lambda i, j: (i, j, 0)) for w in ws],
        out_shape=[jax.ShapeDtypeStruct((b, l, w.shape[1]), F32) for w in ws],
        compiler_params=_cparams(("arbitrary", "arbitrary")),
        name="norm_proj",
    )(x, g.reshape(1, d), shift, scale, *ws)


def _conv_body(x_ref, xm_ref, xp_ref, w_ref, b_ref, o_ref, *, n_l2, n_qscale, hd):
    i = pl.program_id(1)
    j = pl.program_id(2)
    x = x_ref[0]
    r, tc = x.shape
    row = lax.broadcasted_iota(I32, (r, tc), 0)
    prev_row = jnp.where(i == 0, 0.0, xm_ref[0][SUBLANES - 1:SUBLANES, :])
    next_row = jnp.where(i == pl.num_programs(1) - 1, 0.0, xp_ref[0][0:1, :])
    x_prev = jnp.where(row == 0, prev_row, pltpu.roll(x, 1, axis=0))
    x_next = jnp.where(row == r - 1, next_row, pltpu.roll(x, r - 1, axis=0))
    w = w_ref[...]
    y = _silu(x_prev * w[0:1] + x * w[1:2] + x_next * w[2:3] + b_ref[...])
    if n_l2 == 0:
        o_ref[0] = y
        return

    @pl.when(j >= n_l2)
    def _():
        o_ref[0] = y

    @pl.when(j < n_l2)
    def _():
        qs = jnp.where(j < n_qscale, hd ** -0.5, 1.0)
        for k in range(tc // hd):
            yk = y[:, k * hd:(k + 1) * hd]
            ss = jnp.sum(yk * yk, axis=-1, keepdims=True)
            o_ref[0, :, k * hd:(k + 1) * hd] = yk * (lax.rsqrt(ss + EPS) * qs)


def _conv_silu(x, w, bias, n_l2=0, n_qscale=0, hd=GDN_DK):
    b, l, c = x.shape
    r = min(l, 512)
    tc = 512
    rs = r // SUBLANES
    last = l // SUBLANES - 1
    return pl.pallas_call(
        functools.partial(_conv_body, n_l2=n_l2, n_qscale=n_qscale, hd=hd),
        grid=(b, l // r, c // tc),
        in_specs=[
            pl.BlockSpec((1, r, tc), lambda bi, i, j: (bi, i, j)),
            pl.BlockSpec((1, SUBLANES, tc), lambda bi, i, j: (bi, jnp.maximum(i * rs - 1, 0), j)),
            pl.BlockSpec((1, SUBLANES, tc), lambda bi, i, j: (bi, jnp.minimum((i + 1) * rs, last), j)),
            pl.BlockSpec((3, tc), lambda bi, i, j: (0, j)),
            pl.BlockSpec((1, tc), lambda bi, i, j: (0, j)),
        ],
        out_specs=pl.BlockSpec((1, r, tc), lambda bi, i, j: (bi, i, j)),
        out_shape=jax.ShapeDtypeStruct(x.shape, F32),
        compiler_params=_cparams(("arbitrary", "arbitrary", "arbitrary")),
        name="conv_silu",
    )(x, x, x, w, bias.reshape(1, c))


def _chunk_masks(fwd):
    ii = lax.broadcasted_iota(I32, (CHUNK, CHUNK), 0)
    jj = lax.broadcasted_iota(I32, (CHUNK, CHUNK), 1)
    incl = (ii >= jj) if fwd else (ii <= jj)
    strict = (ii > jj) if fwd else (ii < jj)
    return incl, strict


def _cumsum_chunk(incl, v):
    return jnp.dot(incl.astype(F32), v, precision=HIGHEST, preferred_element_type=F32)


def _softplus(x):
    return jnp.maximum(x, 0.0) + jnp.log1p(jnp.exp(-jnp.abs(x)))


def _ssd_chunk(dirs, bias, a, spread_ref, diag_ref):
    units = []
    for d, (x_ref, b_ref, c_ref, dt_ref, s_ref, y_ref, col0, fwd) in enumerate(dirs):
        incl, _ = _chunk_masks(fwd)
        bm_t = b_ref[0].T.astype(BF16)
        cm = c_ref[0].astype(BF16)
        dt = _softplus(dt_ref[0] + bias)
        acs = _cumsum_chunk(incl, dt * a)
        acs_t = acs.T
        tot = acs[CHUNK - 1:CHUNK] if fwd else acs[0:1]
        cols = jnp.concatenate([dt, jnp.exp(tot - acs), jnp.exp(acs),
                                jnp.broadcast_to(jnp.exp(tot), (SUBLANES, LANES))], axis=0)
        units.append(dict(
            x_ref=x_ref, y_ref=y_ref, s_ref=s_ref, bm_t=bm_t, cm=cm, cols=cols, spread=spread_ref[d],
            cb=_dot(cm, bm_t),
            lms=[jnp.exp(jnp.where(incl, acs[:, col0 + k:col0 + k + 1] - acs_t[col0 + k:col0 + k + 1, :], -jnp.inf))
                 for k in range(SSD_HPG)]))
    wide = []
    for u in units:
        h1 = u["cols"].astype(BF16)
        r1 = u["cols"] - h1.astype(F32)
        h2 = r1.astype(BF16)
        h3 = (r1 - h2.astype(F32)).astype(BF16)
        wide.append(_dot(h1, u["spread"]) + _dot(h2, u["spread"]) + _dot(h3, u["spread"]))
    q = CHUNK
    xdt = [u["x_ref"][0] * w[0:q] for u, w in zip(units, wide)]
    m_cat = [jnp.concatenate([u["cb"] * lm for lm in u["lms"]], axis=1).astype(BF16) for u in units]
    y_in = [_dot(m, jnp.tile(xd.astype(BF16), (SSD_HPG, 1)) * diag_ref[...]) for m, xd in zip(m_cat, xdt)]
    ss = [u["s_ref"][...] for u in units]
    y_x = [_dot(u["cm"], s.astype(BF16)) * w[2 * q:3 * q] for u, s, w in zip(units, ss, wide)]
    for u, yi, yx in zip(units, y_in, y_x):
        u["y_ref"][0] = yi + yx
    new = [s * w[3 * q:3 * q + 1] + _dot(u["bm_t"], (xd * w[q:2 * q]).astype(BF16))
           for u, s, w, xd in zip(units, ss, wide, xdt)]
    for u, s in zip(units, new):
        u["s_ref"][...] = s


def _load_states(states, s0_ref):
    for k, s in enumerate(states):
        s[...] = jnp.zeros_like(s) if s0_ref is None else s0_ref[0, k]


def _store_states(out_ref, states):
    for k, s in enumerate(states):
        out_ref[0, k] = s[...]


def _ssd_body(*refs, has_state):
    (xf, bf, cf, dtf, xb, bb, cbk, dtb, bias_ref, alog_ref, spread_ref, diag_ref), refs = refs[:12], refs[12:]
    s0f = s0b = None
    if has_state:
        (s0f, s0b), refs = refs[:2], refs[2:]
    yf, yb, sfo, sbo, sf, sb = refs
    c = pl.program_id(2)
    p = SSD_HEAD_DIM

    @pl.when(c == 0)
    def _():
        for s, s0 in ((sf, s0f), (sb, s0b)):
            for k in range(SSD_HPG):
                s[:, k * p:(k + 1) * p] = jnp.zeros((SSD_STATE, p), F32) if s0 is None else s0[0, k].T

    bias = bias_ref[0]
    a = -jnp.exp(alog_ref[0])
    _ssd_chunk([(xf, bf, cf, dtf, sf, yf, 0, True), (xb, bb, cbk, dtb, sb, yb, SSD_HPG, False)],
               bias, a, spread_ref, diag_ref)

    @pl.when(c == pl.num_programs(2) - 1)
    def _():
        for s, out in ((sf, sfo), (sb, sbo)):
            for k in range(SSD_HPG):
                out[0, k] = s[:, k * p:(k + 1) * p].T


def _ssd_scan(xbc, dt, bias, alog, s0f=None, s0b=None):
    b, l, _ = xbc.shape
    nc = l // CHUNK
    gw = SSD_HPG * SSD_HEAD_DIM
    boff = SSD_INNER // SSD_STATE
    coff = boff + SSD_GROUPS
    has_state = s0f is not None

    def seq_specs(ci):
        return [
            pl.BlockSpec((1, CHUNK, gw), lambda bi, g, c: (bi, ci(c), g)),
            pl.BlockSpec((1, CHUNK, SSD_STATE), lambda bi, g, c: (bi, ci(c), boff + g)),
            pl.BlockSpec((1, CHUNK, SSD_STATE), lambda bi, g, c: (bi, ci(c), coff + g)),
            pl.BlockSpec((1, CHUNK, LANES), lambda bi, g, c: (bi, ci(c), g)),
        ]

    fw = lambda c: c
    bw = lambda c: nc - 1 - c
    st_spec = pl.BlockSpec((1, SSD_HPG, SSD_HEAD_DIM, SSD_STATE), lambda bi, g, c: (bi, g, 0, 0))
    par_spec = pl.BlockSpec((1, 1, LANES), lambda bi, g, c: (g, 0, 0))
    head_of_lane = jnp.arange(gw) // SSD_HEAD_DIM
    lane_col = jnp.arange(LANES)[:, None]
    spread = jnp.stack([lane_col == head_of_lane[None, :], lane_col == head_of_lane[None, :] + SSD_HPG]).astype(BF16)
    diag = (jnp.arange(SSD_HPG * CHUNK)[:, None] // CHUNK == head_of_lane[None, :]).astype(BF16)
    in_specs = seq_specs(fw) + seq_specs(bw) + [par_spec, par_spec, _const_spec(spread.shape), _const_spec(diag.shape)]
    args = [xbc, xbc, xbc, dt, xbc, xbc, xbc, dt, bias, alog, spread, diag]
    if has_state:
        in_specs += [st_spec, st_spec]
        args += [s0f, s0b]
    st_shape = jax.ShapeDtypeStruct((b, SSD_HEADS, SSD_HEAD_DIM, SSD_STATE), F32)
    y_shape = jax.ShapeDtypeStruct((b, l, SSD_INNER), F32)
    return pl.pallas_call(
        functools.partial(_ssd_body, has_state=has_state),
        grid=(b, SSD_GROUPS, nc),
        in_specs=in_specs,
        out_specs=[
            pl.BlockSpec((1, CHUNK, gw), lambda bi, g, c: (bi, c, g)),
            pl.BlockSpec((1, CHUNK, gw), lambda bi, g, c: (bi, nc - 1 - c, g)),
            st_spec, st_spec,
        ],
        out_shape=[y_shape, y_shape, st_shape, st_shape],
        scratch_shapes=[pltpu.VMEM((SSD_STATE, gw), F32)] * 2,
        compiler_params=_cparams(("arbitrary", "arbitrary", "arbitrary")),
        name="ssd_scan",
    )(*args)


def _ssd_out_body(yf_ref, yb_ref, xs_ref, z_ref, d_ref, ng_ref, w_ref, x_ref, gate_ref, o_ref):
    y = yf_ref[0] + yb_ref[0] + xs_ref[0] * d_ref[...]
    y = y * _silu(z_ref[0])
    y = y * lax.rsqrt(jnp.mean(y * y, axis=-1, keepdims=True) + EPS) * ng_ref[...]
    o_ref[0] = x_ref[0] + gate_ref[0] * _dot(y.astype(BF16), w_ref[...])


def _ssd_out(yf, yb, xbc, z, d_skip, norm_g, w_out, x, gate):
    b, l, d = x.shape
    tl = min(l, 256)
    inner = SSD_INNER
    tok = lambda n: pl.BlockSpec((1, tl, n), lambda i, j: (i, j, 0))
    return pl.pallas_call(
        _ssd_out_body,
        grid=(b, l // tl),
        in_specs=[tok(inner), tok(inner), tok(inner), tok(inner),
                  _const_spec((1, inner)), _const_spec((1, inner)), _const_spec(w_out.shape),
                  tok(d), _mod_spec(gate.shape[0])],
        out_specs=tok(d),
        out_shape=jax.ShapeDtypeStruct(x.shape, F32),
        compiler_params=_cparams(("arbitrary", "arbitrary")),
        name="ssd_out",
    )(yf, yb, xbc, z, d_skip, norm_g, w_out, x, gate)


def _mm(a, b):
    return _dot(a.astype(BF16), b.astype(BF16))


def _mm_nt(a, b):
    return _dot_nt(a.astype(BF16), b.astype(BF16))


INV_BASE_LOG2 = 3


def _unit_tri_inverses(ms):
    ii = lax.broadcasted_iota(I32, (CHUNK, CHUNK), 0)
    jj = lax.broadcasted_iota(I32, (CHUNK, CHUNK), 1)
    eye = jnp.where(ii == jj, 1.0, 0.0)
    base = (ii >> INV_BASE_LOG2) == (jj >> INV_BASE_LOG2)
    pws = [jnp.where(base, m, 0.0) for m in ms]
    xs = [eye - d for d in pws]
    for _ in range(INV_BASE_LOG2 - 1):
        pws = [_mm(pw, pw) for pw in pws]
        xs = [x + _mm(x, pw) for x, pw in zip(xs, pws)]
    for lg in range(INV_BASE_LOG2, int(math.log2(CHUNK))):
        off_diag = ((ii >> (lg + 1)) == (jj >> (lg + 1))) & ((ii >> lg) != (jj >> lg))
        cxs = [_mm(jnp.where(off_diag, m, 0.0), x) for m, x in zip(ms, xs)]
        xs = [x - _mm(x, cx) for x, cx in zip(xs, cxs)]
    return xs


def _gdn_chunk(dirs, bias, neg_a):
    dk, dv = GDN_DK, GDN_DV
    units = []
    for q_ref, k_ref, v_ref, ab_ref, states, o_ref, off, fwd in dirs:
        incl, strict = _chunk_masks(fwd)
        q = q_ref[0]
        k = k_ref[0]
        v = v_ref[0]
        ab = ab_ref[0]
        beta_all = jax.nn.sigmoid(ab)
        gc = _cumsum_chunk(incl, neg_a * _softplus(ab + bias))
        gc_t = gc.T
        tot = gc[CHUNK - 1:CHUNK] if fwd else gc[0:1]
        e_gc = jnp.exp(gc)
        e_rest = jnp.exp(tot - gc)
        g_last = jnp.exp(tot)
        for h in range(GDN_HPG):
            cb = off + h
            ca = 2 * GDN_HPG + off + h
            kh = k[:, h * dk:(h + 1) * dk]
            beta = beta_all[:, cb:cb + 1]
            units.append(dict(
                incl=incl, strict=strict, kh=kh, kb=kh * beta,
                qh=q[:, h * dk:(h + 1) * dk], vb=v[:, h * dv:(h + 1) * dv] * beta,
                decay=jnp.exp(jnp.where(incl, gc[:, ca:ca + 1] - gc_t[ca:ca + 1, :], -jnp.inf)),
                e_gc=e_gc[:, ca:ca + 1], e_rest=e_rest[:, ca:ca + 1], g_last=g_last[:, ca:ca + 1],
                s_ref=states[h], o_ref=o_ref, cols=slice(h * dv, (h + 1) * dv)))
    ms = [jnp.where(u["strict"], _mm_nt(u["kb"], u["kh"]) * u["decay"], 0.0) for u in units]
    ts = _unit_tri_inverses(ms)
    us = [_mm(t, u["vb"]) for t, u in zip(ts, units)]
    ws = [_mm(t, u["kb"] * u["e_gc"]) for t, u in zip(ts, units)]
    aqk = [jnp.where(u["incl"], _mm_nt(u["qh"], u["kh"]) * u["decay"], 0.0) for u in units]
    ss = [u["s_ref"][...] for u in units]
    vn = [x - _mm(w, s) for x, w, s in zip(us, ws, ss)]
    os_ = [_mm(u["qh"] * u["e_gc"], s) + _mm(a, v) for u, s, a, v in zip(units, ss, aqk, vn)]
    for u, o in zip(units, os_):
        u["o_ref"][0, :, u["cols"]] = o
    new = [s * u["g_last"] + _mm((u["kh"] * u["e_rest"]).T, v) for u, s, v in zip(units, ss, vn)]
    for u, s in zip(units, new):
        u["s_ref"][...] = s


def _gdn_body(*refs, has_state):
    (qf, kf, vf, abf, qb, kb, vb, abb, bias_ref, alog_ref), refs = refs[:10], refs[10:]
    s0f = s0b = None
    if has_state:
        (s0f, s0b), refs = refs[:2], refs[2:]
    (of, ob, sfo, sbo), states = refs[:4], refs[4:]
    sf, sb = states[:GDN_HPG], states[GDN_HPG:]
    c = pl.program_id(2)

    @pl.when(c == 0)
    def _():
        _load_states(sf, s0f)
        _load_states(sb, s0b)

    bias = bias_ref[0]
    neg_a = -jnp.exp(alog_ref[0])
    _gdn_chunk([(qf, kf, vf, abf, sf, of, 0, True), (qb, kb, vb, abb, sb, ob, GDN_HPG, False)], bias, neg_a)

    @pl.when(c == pl.num_programs(2) - 1)
    def _():
        _store_states(sfo, sf)
        _store_states(sbo, sb)


def _gdn_scan(qkv, ab, bias, alog, s0f=None, s0b=None):
    b, l, _ = qkv.shape
    nc = l // CHUNK
    qw = GDN_HPG * GDN_DK
    vw = GDN_HPG * GDN_DV
    koff = GDN_QK // qw
    voff = 2 * GDN_QK // vw
    has_state = s0f is not None

    def seq_specs(ci):
        return [
            pl.BlockSpec((1, CHUNK, qw), lambda bi, g, c: (bi, ci(c), g)),
            pl.BlockSpec((1, CHUNK, qw), lambda bi, g, c: (bi, ci(c), koff + g)),
            pl.BlockSpec((1, CHUNK, vw), lambda bi, g, c: (bi, ci(c), voff + g)),
            pl.BlockSpec((1, CHUNK, LANES), lambda bi, g, c: (bi, ci(c), g)),
        ]

    fw = lambda c: c
    bw = lambda c: nc - 1 - c
    st_spec = pl.BlockSpec((1, GDN_HPG, GDN_DK, GDN_DV), lambda bi, g, c: (bi, g, 0, 0))
    par_spec = pl.BlockSpec((1, 1, LANES), lambda bi, g, c: (g, 0, 0))
    in_specs = seq_specs(fw) + seq_specs(bw) + [par_spec, par_spec]
    args = [qkv, qkv, qkv, ab, qkv, qkv, qkv, ab, bias, alog]
    if has_state:
        in_specs += [st_spec, st_spec]
        args += [s0f, s0b]
    st_shape = jax.ShapeDtypeStruct((b, GDN_HEADS, GDN_DK, GDN_DV), F32)
    o_shape = jax.ShapeDtypeStruct((b, l, GDN_VW), F32)
    return pl.pallas_call(
        functools.partial(_gdn_body, has_state=has_state),
        grid=(b, GDN_GROUPS, nc),
        in_specs=in_specs,
        out_specs=[
            pl.BlockSpec((1, CHUNK, vw), lambda bi, g, c: (bi, c, g)),
            pl.BlockSpec((1, CHUNK, vw), lambda bi, g, c: (bi, nc - 1 - c, g)),
            st_spec, st_spec,
        ],
        out_shape=[o_shape, o_shape, st_shape, st_shape],
        scratch_shapes=[pltpu.VMEM((GDN_DK, GDN_DV), F32)] * (2 * GDN_HPG),
        compiler_params=_cparams(("arbitrary", "arbitrary", "arbitrary")),
        name="gdn_scan",
    )(*args)


def _gdn_out_body(of_ref, ob_ref, z_ref, ng_ref, w_ref, x_ref, gate_ref, o_ref, y_ref):
    dv = GDN_DV
    for h in range(GDN_HEADS):
        sl = slice(h * dv, (h + 1) * dv)
        o = of_ref[0, :, sl] + ob_ref[0, :, sl]
        o = o * lax.rsqrt(jnp.mean(o * o, axis=-1, keepdims=True) + EPS) * ng_ref[...]
        y_ref[:, sl] = (o * _silu(z_ref[0, :, sl])).astype(BF16)
    o_ref[0] = x_ref[0] + gate_ref[0] * _dot(y_ref[...], w_ref[...])


def _gdn_out(of, ob, z, norm_g, w_out, x, gate):
    b, l, d = x.shape
    tl = min(l, 256)
    tok = lambda n: pl.BlockSpec((1, tl, n), lambda i, j: (i, j, 0))
    return pl.pallas_call(
        _gdn_out_body,
        grid=(b, l // tl),
        in_specs=[tok(GDN_VW), tok(GDN_VW), tok(GDN_VW),
                  _const_spec((1, GDN_DV)), _const_spec(w_out.shape),
                  tok(d), _mod_spec(gate.shape[0])],
        out_specs=tok(d),
        out_shape=jax.ShapeDtypeStruct(x.shape, F32),
        scratch_shapes=[pltpu.VMEM((tl, GDN_VW), BF16)],
        compiler_params=_cparams(("arbitrary", "arbitrary")),
        name="gdn_out",
    )(of, ob, z, norm_g, w_out, x, gate)


def _topk_rows(s, k, ids=None):
    if ids is None:
        ids = lax.broadcasted_iota(I32, s.shape, 0)
        fill = s.shape[0] - 1
    else:
        fill = N_EXPERTS
    vals, idxs = [], []
    for _ in range(k):
        m = jnp.max(s, axis=0, keepdims=True)
        idx = jnp.min(jnp.where(s == m, ids, fill), axis=0, keepdims=True)
        vals.append(m)
        idxs.append(idx)
        s = jnp.where(ids == idx, -jnp.inf, s)
    return jnp.concatenate(vals, axis=0), jnp.concatenate(idxs, axis=0)


def _pair_candidates(t):
    k = PEER_TOPK
    r = lax.broadcasted_iota(I32, (k + 7 * SUBLANES + SUBLANES, t), 0)
    mid = r - k
    a_mid = (mid >> 3) + 1
    b_mid = mid & (SUBLANES - 1)
    tail = r - (k + 7 * SUBLANES)
    ids = jnp.where(r < k, r, jnp.where(tail < 0, a_mid * k + b_mid, (tail + SUBLANES) * k))
    valid = (r < k) | (tail >= 0) | ((a_mid + 1) * (b_mid + 1) <= k)
    return ids, valid


def _pick_rows(sel, table):
    out = jnp.zeros(sel.shape, table.dtype)
    for a in range(table.shape[0]):
        out = jnp.where(sel == a, table[a:a + 1], out)
    return out


def _peer_select_body(x_ref, g_ref, sh_ref, sc_ref, wqh_ref, wql_ref, kh_ref, kl_ref,
                      h_ref, eid_ref, gate_ref):
    h = _adaln(x_ref[0], g_ref[...], sh_ref[0], sc_ref[0])
    h_ref[0] = h
    hh, hl = _split_bf16(h)
    q = _dot(hh, wqh_ref[...]) + _dot(hh, wql_ref[...]) + _dot(hl, wqh_ref[...])
    eids, gates = [], []
    cand_ids, cand_ok = _pair_candidates(h.shape[0])
    for hd in range(PEER_HEADS):
        sv, si = [], []
        for s in range(2):
            c0 = hd * PEER_DKEY + s * PEER_HALF
            qh, ql = _split_bf16(q[:, c0:c0 + PEER_HALF])
            kh = kh_ref[s, hd]
            st = _dot_nt(kh, qh) + _dot_nt(kh, ql) + _dot_nt(kl_ref[s, hd], qh)
            v, i = _topk_rows(st, PEER_TOPK)
            sv.append(v)
            si.append(i)
        comb = jnp.concatenate(
            [sv[0][0:1] + sv[1]]
            + [sv[0][a:a + 1] + sv[1][0:SUBLANES] for a in range(1, SUBLANES)]
            + [sv[0][SUBLANES:] + sv[1][0:1]], axis=0)
        cv, ci = _topk_rows(jnp.where(cand_ok, comb, -jnp.inf), PEER_TOPK, cand_ids)
        shift = int(math.log2(PEER_TOPK))
        i1 = _pick_rows(ci >> shift, si[0])
        i2 = _pick_rows(ci & (PEER_TOPK - 1), si[1])
        eids.append(i1 * N_KEYS + i2)
        e = jnp.exp(cv - cv[0:1])
        gates.append(e / jnp.sum(e, axis=0, keepdims=True))
    eid_ref[0] = jnp.concatenate(eids, axis=0).astype(F32).T.astype(I32)
    gate_ref[0] = jnp.concatenate(gates, axis=0).T


def _peer_select(x, g, shift, scale, wq_hi, wq_lo, k_hi, k_lo):
    b, l, d = x.shape
    tl = min(l, 256)
    bm = shift.shape[0]
    tok = lambda n: pl.BlockSpec((1, tl, n), lambda i, j: (i, j, 0))
    return pl.pallas_call(
        _peer_select_body,
        grid=(b, l // tl),
        in_specs=[tok(d), _const_spec((1, d)), _mod_spec(bm), _mod_spec(bm),
                  _const_spec(wq_hi.shape), _const_spec(wq_lo.shape),
                  _const_spec(k_hi.shape), _const_spec(k_lo.shape)],
        out_specs=[tok(d), tok(PEER_SLOTS), tok(PEER_SLOTS)],
        out_shape=[jax.ShapeDtypeStruct((b, l, d), F32),
                   jax.ShapeDtypeStruct((b, l, PEER_SLOTS), I32),
                   jax.ShapeDtypeStruct((b, l, PEER_SLOTS), F32)],
        compiler_params=_cparams(("arbitrary", "arbitrary")),
        name="peer_select",
    )(x, g.reshape(1, d), shift, scale, wq_hi, wq_lo, k_hi, k_lo)


def _pack_body(t_ref, o_ref):
    t = t_ref[...]
    half = t.shape[1] // 2
    lo = lax.bitcast_convert_type(t[:, :half].astype(BF16).astype(F32), jnp.uint32)
    hi = lax.bitcast_convert_type(t[:, half:].astype(BF16).astype(F32), jnp.uint32)
    word = (hi & jnp.uint32(0xFFFF0000)) | (lo >> 16)
    o_ref[...] = lax.bitcast_convert_type(word, I32)


def _pack_table(tab):
    e, d = tab.shape
    r = 512
    out = pl.pallas_call(
        _pack_body,
        grid=(e // r,),
        in_specs=[pl.BlockSpec((r, d), lambda i: (i, 0))],
        out_specs=pl.BlockSpec((r, d // 2), lambda i: (i, 0)),
        out_shape=jax.ShapeDtypeStruct((e, d // 2), I32),
        compiler_params=_cparams(("arbitrary",)),
        name="pack_table",
    )(tab)
    return out.reshape(e, HALF_ROWS, LANES)


def _unpack_words(w):
    lo = lax.bitcast_convert_type(w << 16, F32)
    hi = lax.bitcast_convert_type(w & jnp.int32(-65536), F32)
    return lo, hi


PEER_LAG = 2


def _peer_u_body(eid_ref, hr_ref, gate_ref, tab_ref, act_ref, ps_a, ps_b, acc_ref, *, tb):
    ones = jnp.ones((LANES, LANES), BF16)
    lane = lax.broadcasted_iota(I32, (PEER_SLOTS, tb), 1)

    def gather(t, ps_ref):
        xr = hr_ref[0, pl.ds(pl.multiple_of(t * ROW_VREGS, ROW_VREGS), ROW_VREGS), :]
        x_lo = xr[0:HALF_ROWS]
        x_hi = xr[HALF_ROWS:ROW_VREGS]
        for j in range(PEER_SLOTS):
            lo, hi = _unpack_words(tab_ref[eid_ref[0, t, j]])
            ps_ref[j * HALF_ROWS:(j + 1) * HALF_ROWS, :] = lo * x_lo + hi * x_hi

    def reduce(ps_ref, tok):
        part = ps_ref[pl.ds(0, PEER_SLOTS, stride=HALF_ROWS), :]
        for s in range(1, HALF_ROWS):
            part = part + ps_ref[pl.ds(s, PEER_SLOTS, stride=HALF_ROWS), :]
        p_hi, p_lo = _split_bf16(part)
        sums = _dot(p_hi, ones) + _dot(p_lo, ones)
        acc_ref[...] = jnp.where(lane == tok, sums, acc_ref[...])

    ps_a[...] = jnp.zeros_like(ps_a)
    ps_b[...] = jnp.zeros_like(ps_b)
    acc_ref[...] = jnp.zeros_like(acc_ref)

    def pair(i, carry):
        t = i * PEER_LAG
        reduce(ps_a, t - PEER_LAG)
        reduce(ps_b, t - PEER_LAG + 1)
        gather(t, ps_a)
        gather(t + 1, ps_b)
        return carry

    lax.fori_loop(0, tb // PEER_LAG, pair, 0)
    reduce(ps_a, tb - PEER_LAG)
    reduce(ps_b, tb - PEER_LAG + 1)
    a = acc_ref[...].T
    act_ref[0] = 0.5 * a * (1.0 + lax.erf(a * (2.0 ** -0.5))) * gate_ref[0]


def _peer_v_body(eid_ref, act_ref, xr_ref, gate_ref, eye_ref, tab_ref, o_ref, wb_a, wb_b, *, tb):
    ones = jnp.ones((PEER_SLOTS, LANES), BF16)
    gate = gate_ref[0]

    def spread(t, wb_ref):
        a = act_ref[0, pl.ds(t, 1), :]
        diag = jnp.where(eye_ref[...] != 0.0, a, 0.0)
        wb_ref[...] = _dot(diag.astype(BF16), ones)

    def combine(t, wb_ref):
        n_acc = 2
        acc_lo = [jnp.zeros((HALF_ROWS, LANES), F32) for _ in range(n_acc)]
        acc_hi = [jnp.zeros((HALF_ROWS, LANES), F32) for _ in range(n_acc)]
        for j in range(PEER_SLOTS):
            lo, hi = _unpack_words(tab_ref[eid_ref[0, t, j]])
            wj = wb_ref[j:j + 1, :]
            acc_lo[j % n_acc] = acc_lo[j % n_acc] + lo * wj
            acc_hi[j % n_acc] = acc_hi[j % n_acc] + hi * wj
        rows = pl.ds(pl.multiple_of(t * ROW_VREGS, ROW_VREGS), ROW_VREGS)
        out = jnp.concatenate([acc_lo[0] + acc_lo[1], acc_hi[0] + acc_hi[1]], axis=0)
        o_ref[0, rows, :] = xr_ref[0, rows, :] + gate * out

    spread(0, wb_a)

    def pair(i, carry):
        t = i * PEER_LAG
        spread(t + 1, wb_b)
        combine(t, wb_a)
        spread(jnp.minimum(t + PEER_LAG, tb - 1), wb_a)
        combine(t + 1, wb_b)
        return carry

    lax.fori_loop(0, tb // PEER_LAG, pair, 0)


def _peer_retrieve(x, h, eid, gate_w, mod_gate, tab_u, tab_v):
    b, l, d = x.shape
    tb = 128
    bm = mod_gate.shape[0]
    xr = x.reshape(b, l * ROW_VREGS, LANES)
    hr = h.reshape(b, l * ROW_VREGS, LANES)
    gate_rows = mod_gate.reshape(bm, ROW_VREGS, LANES)
    tok = pl.BlockSpec((1, tb, PEER_SLOTS), lambda i, j: (i, j, 0))
    eid_spec = pl.BlockSpec((1, tb, PEER_SLOTS), lambda i, j: (i, j, 0), memory_space=pltpu.SMEM)
    row_spec = pl.BlockSpec((1, tb * ROW_VREGS, LANES), lambda i, j: (i, j, 0))
    tab_spec = pl.BlockSpec(memory_space=pltpu.VMEM)
    if bm == 1:
        gate_spec = pl.BlockSpec((1, ROW_VREGS, LANES), lambda i, j: (0, 0, 0))
    else:
        gate_spec = pl.BlockSpec((1, ROW_VREGS, LANES), lambda i, j: (i, 0, 0))
    assert tb == LANES
    eye = jnp.eye(PEER_SLOTS, dtype=F32)
    part = pltpu.VMEM((PEER_SLOTS * HALF_ROWS, LANES), F32)
    act = pl.pallas_call(
        functools.partial(_peer_u_body, tb=tb),
        grid=(b, l // tb),
        in_specs=[eid_spec, row_spec, tok, tab_spec],
        out_specs=tok,
        out_shape=jax.ShapeDtypeStruct((b, l, PEER_SLOTS), F32),
        scratch_shapes=[part, part, pltpu.VMEM((PEER_SLOTS, tb), F32)],
        compiler_params=_cparams(("arbitrary", "arbitrary")),
        name="peer_u",
    )(eid, hr, gate_w, tab_u)
    spread = pltpu.VMEM((PEER_SLOTS, LANES), F32)
    out = pl.pallas_call(
        functools.partial(_peer_v_body, tb=tb),
        grid=(b, l // tb),
        in_specs=[eid_spec, tok, row_spec, gate_spec, _const_spec(eye.shape), tab_spec],
        out_specs=row_spec,
        out_shape=jax.ShapeDtypeStruct(xr.shape, F32),
        scratch_shapes=[spread, spread],
        compiler_params=_cparams(("arbitrary", "arbitrary")),
        name="peer_v",
    )(eid, act, xr, gate_rows, eye, tab_v)
    return out.reshape(b, l, d)


def _final_norm_body(x_ref, g_ref, o_ref):
    x = x_ref[0]
    o_ref[0] = x * lax.rsqrt(jnp.mean(x * x, axis=-1, keepdims=True) + EPS) * g_ref[...]


def _final_norm(x, g):
    b, l, d = x.shape
    tl = min(l, 512)
    return pl.pallas_call(
        _final_norm_body,
        grid=(b, l // tl),
        in_specs=[pl.BlockSpec((1, tl, d), lambda i, j: (i, j, 0)), _const_spec((1, d))],
        out_specs=pl.BlockSpec((1, tl, d), lambda i, j: (i, j, 0)),
        out_shape=jax.ShapeDtypeStruct(x.shape, F32),
        compiler_params=_cparams(("arbitrary", "arbitrary")),
        name="final_norm",
    )(x, g.reshape(1, d))


def _group_blocked(cols, per_group, groups):
    d = cols[0].shape[0]
    blocks = []
    for g in range(groups):
        parts = [c[:, g * per_group:(g + 1) * per_group] for c in cols]
        used = per_group * len(cols)
        blocks.append(jnp.concatenate(parts + [jnp.zeros((d, LANES - used), cols[0].dtype)], axis=1))
    return jnp.concatenate(blocks, axis=1)


def _group_blocked_vec(vecs, per_group, groups):
    rows = []
    for g in range(groups):
        parts = [jnp.zeros((per_group,), F32) if v is None else v[g * per_group:(g + 1) * per_group]
                 for v in vecs]
        used = per_group * len(vecs)
        rows.append(jnp.concatenate(parts + [jnp.zeros((LANES - used,), F32)]))
    return jnp.stack(rows).reshape(groups, 1, LANES)


def _ssd_params(w_in, conv_w, conv_b, dt_bias, a_log, d_skip, norm_g, w_out):
    dt0 = SSD_INNER + SSD_CONV_DIM
    w_dt = _group_blocked([w_in[:, dt0:dt0 + SSD_HEADS], w_in[:, dt0 + SSD_HEADS:]], SSD_HPG, SSD_GROUPS)
    return dict(
        ws=[w_in[:, :SSD_INNER].astype(BF16), w_in[:, SSD_INNER:dt0].astype(BF16), w_dt.astype(BF16)],
        conv_w=conv_w, conv_b=conv_b,
        bias=_group_blocked_vec([dt_bias[0], dt_bias[1]], SSD_HPG, SSD_GROUPS),
        alog=_group_blocked_vec([a_log[0], a_log[1]], SSD_HPG, SSD_GROUPS),
        d_skip=jnp.repeat(d_skip, SSD_HEAD_DIM).reshape(1, SSD_INNER),
        norm_g=norm_g.reshape(1, SSD_INNER),
        w_out=w_out.astype(BF16),
    )


def _gdn_params(w_in, conv_w, dt_bias, a_log, norm_g, w_out):
    ab0 = GDN_CONV_DIM + GDN_VW
    h = GDN_HEADS
    ab = [w_in[:, ab0 + i * h:ab0 + (i + 1) * h] for i in range(4)]
    return dict(
        ws=[w_in[:, :GDN_CONV_DIM].astype(BF16), w_in[:, GDN_CONV_DIM:ab0].astype(BF16),
            _group_blocked(ab, GDN_HPG, GDN_GROUPS).astype(BF16)],
        conv_w=conv_w,
        bias=_group_blocked_vec([None, None, dt_bias[0], dt_bias[1]], GDN_HPG, GDN_GROUPS),
        alog=_group_blocked_vec([None, None, a_log[0], a_log[1]], GDN_HPG, GDN_GROUPS),
        norm_g=norm_g.reshape(1, GDN_DV),
        w_out=w_out.astype(BF16),
    )


def _peer_params(w_q, keys, u_tab, v_tab):
    wq_hi, wq_lo = _split_bf16(w_q)
    k_hi, k_lo = _split_bf16(keys)
    return dict(wq_hi=wq_hi, wq_lo=wq_lo, k_hi=k_hi, k_lo=k_lo,
                tab_u=_pack_table(u_tab), tab_v=_pack_table(v_tab))


def _ssd_layer(x, m, g, p, s0f=None, s0b=None):
    z, xbc, dt = _norm_proj(x, g, m[0], m[1], p["ws"])
    xbc = _conv_silu(xbc, p["conv_w"], p["conv_b"])
    yf, yb, sf, sb = _ssd_scan(xbc, dt, p["bias"], p["alog"], s0f, s0b)
    x = _ssd_out(yf, yb, xbc, z, p["d_skip"], p["norm_g"], p["w_out"], x, m[2])
    return x, sf, sb


def _gdn_layer(x, m, g, p, s0f=None, s0b=None):
    qkv, z, ab = _norm_proj(x, g, m[0], m[1], p["ws"])
    qkv = _conv_silu(qkv, p["conv_w"], jnp.zeros((GDN_CONV_DIM,), F32),
                     n_l2=2 * GDN_QK // 512, n_qscale=GDN_QK // 512)
    of, ob, sf, sb = _gdn_scan(qkv, ab, p["bias"], p["alog"], s0f, s0b)
    x = _gdn_out(of, ob, z, p["norm_g"], p["w_out"], x, m[2])
    return x, sf, sb


def _peer_layer(x, m, g, p):
    h, eid, gate_w = _peer_select(x, g, m[3], m[4], p["wq_hi"], p["wq_lo"], p["k_hi"], p["k_lo"])
    return _peer_retrieve(x, h, eid, gate_w, m[5], p["tab_u"], p["tab_v"])


def kernel(x_prompt, x_sample, c, state_ssd_fwd, state_ssd_bwd, state_gdn_fwd, state_gdn_bwd, c_ctx, w_mod, b_mod, norm_mix_g, norm_ffn_g, ssd_w_in, ssd_conv_w, ssd_conv_b, ssd_dt_bias, ssd_a_log, ssd_d, ssd_norm_g, ssd_w_out, gdn_w_in, gdn_conv_w, gdn_dt_bias, gdn_a_log, gdn_norm_g, gdn_w_out, peer_w_q, peer_keys, peer_u, peer_v, final_norm_g):
    d = D_MODEL
    nb = c.shape[0]
    rows = 2 * SUBLANES
    c_all = jnp.zeros((rows, d), F32).at[0].set(c_ctx).at[1:1 + nb].set(c)
    mod = _modulation(c_all, w_mod, b_mod)
    xp = x_prompt
    xs = _add_pos(x_sample, _grid_pos_embed(x_sample.shape[1]))
    ssd_f, ssd_b, gdn_f, gdn_b = [], [], [], []
    for i in range(DEPTH):
        mp = [mod[i, 0:1, k * d:(k + 1) * d].reshape(1, 1, d) for k in range(6)]
        ms = [mod[i, 1:1 + nb, k * d:(k + 1) * d].reshape(nb, 1, d) for k in range(6)]
        j = i // 2
        if i % 2 == 0:
            p = _ssd_params(ssd_w_in[j], ssd_conv_w[j], ssd_conv_b[j], ssd_dt_bias[j], ssd_a_log[j],
                            ssd_d[j], ssd_norm_g[j], ssd_w_out[j])
            xp, sf, sb = _ssd_layer(xp, mp, norm_mix_g[i], p)
            xs, _, _ = _ssd_layer(xs, ms, norm_mix_g[i], p, state_ssd_fwd[:, j], state_ssd_bwd[:, j])
            ssd_f.append(sf)
            ssd_b.append(sb)
        else:
            p = _gdn_params(gdn_w_in[j], gdn_conv_w[j], gdn_dt_bias[j], gdn_a_log[j], gdn_norm_g[j], gdn_w_out[j])
            xp, sf, sb = _gdn_layer(xp, mp, norm_mix_g[i], p)
            xs, _, _ = _gdn_layer(xs, ms, norm_mix_g[i], p, state_gdn_fwd[:, j], state_gdn_bwd[:, j])
            gdn_f.append(sf)
            gdn_b.append(sb)
        pp = _peer_params(peer_w_q[i], peer_keys[i], peer_u[i], peer_v[i])
        xp = _peer_layer(xp, mp, norm_ffn_g[i], pp)
        xs = _peer_layer(xs, ms, norm_ffn_g[i], pp)
    y_prompt = _final_norm(xp, final_norm_g)
    y_sample = _final_norm(xs, final_norm_g)
    return (y_prompt, y_sample, jnp.stack(ssd_f, axis=1), jnp.stack(ssd_b, axis=1),
            jnp.stack(gdn_f, axis=1), jnp.stack(gdn_b, axis=1))
```

```python
import functools
import math

import jax
import jax.numpy as jnp
from jax import lax
from jax.experimental import pallas as pl
from jax.experimental.pallas import tpu as pltpu

F32 = jnp.float32
BF16 = jnp.bfloat16
I32 = jnp.int32
HIGHEST = lax.Precision.HIGHEST

D_MODEL = 1024
DEPTH = 2
GRID_W = 64
CHUNK = 64
EPS = 1e-6
SSD_INNER = 2 * D_MODEL
SSD_HEAD_DIM = 64
SSD_HEADS = SSD_INNER // SSD_HEAD_DIM
SSD_GROUPS = 4
SSD_HPG = SSD_HEADS // SSD_GROUPS
SSD_STATE = 128
SSD_GN = SSD_GROUPS * SSD_STATE
SSD_CONV_DIM = SSD_INNER + 2 * SSD_GN
GDN_HEADS = 8
GDN_DK = 128
GDN_DV = 256
GDN_QK = GDN_HEADS * GDN_DK
GDN_VW = GDN_HEADS * GDN_DV
GDN_CONV_DIM = 2 * GDN_QK + GDN_VW
GDN_HPG = 4
GDN_GROUPS = GDN_HEADS // GDN_HPG
PEER_HEADS = 8
N_KEYS = 128
N_EXPERTS = N_KEYS * N_KEYS
PEER_TOPK = 16
PEER_DKEY = 256
PEER_HALF = PEER_DKEY // 2
PEER_SLOTS = PEER_HEADS * PEER_TOPK

LANES = 128
SUBLANES = 8
ROW_VREGS = D_MODEL // LANES
HALF_ROWS = ROW_VREGS // 2
VMEM_LIMIT = 56 * 1024 * 1024


def _cparams(sem):
    return pltpu.CompilerParams(dimension_semantics=sem, vmem_limit_bytes=VMEM_LIMIT)


def _const_spec(shape):
    nd = len(shape)
    return pl.BlockSpec(shape, lambda *_: (0,) * nd, pipeline_mode=pl.Buffered(1))


def _mod_spec(bm):
    if bm == 1:
        return pl.BlockSpec((1, 1, D_MODEL), lambda b, *_: (0, 0, 0))
    return pl.BlockSpec((1, 1, D_MODEL), lambda b, *_: (b, 0, 0))


def _silu(x):
    return x * jax.nn.sigmoid(x)


def _split_bf16(a):
    hi = a.astype(BF16)
    lo = (a - hi.astype(F32)).astype(BF16)
    return hi, lo


def _dot(a, b):
    return jnp.dot(a, b, preferred_element_type=F32)


def _dot_nt(a, b):
    return lax.dot_general(a, b, (((1,), (1,)), ((), ())), preferred_element_type=F32)


def _adaln(x, g, shift, scale):
    ms = jnp.mean(x * x, axis=-1, keepdims=True)
    return x * lax.rsqrt(ms + EPS) * g * (1.0 + scale) + shift


def _mod_body(c_ref, w_ref, b_ref, o_ref):
    s = _silu(c_ref[...])
    o_ref[0] = jnp.dot(s, w_ref[0], precision=HIGHEST, preferred_element_type=F32) + b_ref[0]


def _modulation(c_all, w_mod, b_mod):
    depth, d, n = w_mod.shape
    rows = c_all.shape[0]
    tn = 1536
    return pl.pallas_call(
        _mod_body,
        grid=(depth, n // tn),
        in_specs=[
            pl.BlockSpec((rows, d), lambda i, j: (0, 0)),
            pl.BlockSpec((1, d, tn), lambda i, j: (i, 0, j)),
            pl.BlockSpec((1, 1, tn), lambda i, j: (i, 0, j)),
        ],
        out_specs=pl.BlockSpec((1, rows, tn), lambda i, j: (i, 0, j)),
        out_shape=jax.ShapeDtypeStruct((depth, rows, n), F32),
        compiler_params=_cparams(("arbitrary", "arbitrary")),
        name="modulation",
    )(c_all, w_mod, b_mod.reshape(depth, 1, n))


def _grid_pos_embed(n):
    rows = n // GRID_W
    r = jnp.repeat(jnp.arange(rows, dtype=F32), GRID_W)
    col = jnp.tile(jnp.arange(GRID_W, dtype=F32), rows)
    quarter = D_MODEL // 4
    omega = 1.0 / (10000.0 ** (jnp.arange(quarter, dtype=F32) / quarter))

    def enc(p):
        ang = p[:, None] * omega[None, :]
        return jnp.concatenate([jnp.sin(ang), jnp.cos(ang)], axis=-1)

    return jnp.concatenate([enc(r), enc(col)], axis=-1)


def _add_body(x_ref, p_ref, o_ref):
    o_ref[0] = x_ref[0] + p_ref[...]


def _add_pos(x, pe):
    b, l, d = x.shape
    tl = 512
    return pl.pallas_call(
        _add_body,
        grid=(l // tl, b),
        in_specs=[pl.BlockSpec((1, tl, d), lambda i, j: (j, i, 0)),
                  pl.BlockSpec((tl, d), lambda i, j: (i, 0))],
        out_specs=pl.BlockSpec((1, tl, d), lambda i, j: (j, i, 0)),
        out_shape=jax.ShapeDtypeStruct(x.shape, F32),
        compiler_params=_cparams(("arbitrary", "arbitrary")),
        name="add_pos",
    )(x, pe)


def _norm_proj_body(x_ref, g_ref, sh_ref, sc_ref, *refs, n_w):
    h = _adaln(x_ref[0], g_ref[...], sh_ref[0], sc_ref[0]).astype(BF16)
    for w_ref, o_ref in zip(refs[:n_w], refs[n_w:]):
        o_ref[0] = _dot(h, w_ref[...])


def _norm_proj(x, g, shift, scale, ws):
    b, l, d = x.shape
    tl = min(l, 256)
    bm = shift.shape[0]
    in_specs = [
        pl.BlockSpec((1, tl, d), lambda i, j: (i, j, 0)),
        _const_spec((1, d)),
        _mod_spec(bm),
        _mod_spec(bm),
    ] + [_const_spec(w.shape) for w in ws]
    return pl.pallas_call(
        functools.partial(_norm_proj_body, n_w=len(ws)),
        grid=(b, l // tl),
        in_specs=in_specs,
        out_specs=[pl.BlockSpec((1, tl, w.shape[1]), lambda i, j: (i, j, 0)) for w in ws],
        out_shape=[jax.ShapeDtypeStruct((b, l, w.shape[1]), F32) for w in ws],
        compiler_params=_cparams(("arbitrary", "arbitrary")),
        name="norm_proj",
    )(x, g.reshape(1, d), shift, scale, *ws)


def _conv_body(x_ref, xm_ref, xp_ref, w_ref, b_ref, o_ref, *, n_l2, n_qscale, hd):
    i = pl.program_id(1)
    j = pl.program_id(2)
    x = x_ref[0]
    r, tc = x.shape
    row = lax.broadcasted_iota(I32, (r, tc), 0)
    prev_row = jnp.where(i == 0, 0.0, xm_ref[0][SUBLANES - 1:SUBLANES, :])
    next_row = jnp.where(i == pl.num_programs(1) - 1, 0.0, xp_ref[0][0:1, :])
    x_prev = jnp.where(row == 0, prev_row, pltpu.roll(x, 1, axis=0))
    x_next = jnp.where(row == r - 1, next_row, pltpu.roll(x, r - 1, axis=0))
    w = w_ref[...]
    y = _silu(x_prev * w[0:1] + x * w[1:2] + x_next * w[2:3] + b_ref[...])
    if n_l2 == 0:
        o_ref[0] = y
        return

    @pl.when(j >= n_l2)
    def _():
        o_ref[0] = y

    @pl.when(j < n_l2)
    def _():
        qs = jnp.where(j < n_qscale, hd ** -0.5, 1.0)
        for k in range(tc // hd):
            yk = y[:, k * hd:(k + 1) * hd]
            ss = jnp.sum(yk * yk, axis=-1, keepdims=True)
            o_ref[0, :, k * hd:(k + 1) * hd] = yk * (lax.rsqrt(ss + EPS) * qs)


def _conv_silu(x, w, bias, n_l2=0, n_qscale=0, hd=GDN_DK):
    b, l, c = x.shape
    r = min(l, 512)
    tc = 512
    rs = r // SUBLANES
    last = l // SUBLANES - 1
    return pl.pallas_call(
        functools.partial(_conv_body, n_l2=n_l2, n_qscale=n_qscale, hd=hd),
        grid=(b, l // r, c // tc),
        in_specs=[
            pl.BlockSpec((1, r, tc), lambda bi, i, j: (bi, i, j)),
            pl.BlockSpec((1, SUBLANES, tc), lambda bi, i, j: (bi, jnp.maximum(i * rs - 1, 0), j)),
            pl.BlockSpec((1, SUBLANES, tc), lambda bi, i, j: (bi, jnp.minimum((i + 1) * rs, last), j)),
            pl.BlockSpec((3, tc), lambda bi, i, j: (0, j)),
            pl.BlockSpec((1, tc), lambda bi, i, j: (0, j)),
        ],
        out_specs=pl.BlockSpec((1, r, tc), lambda bi, i, j: (bi, i, j)),
        out_shape=jax.ShapeDtypeStruct(x.shape, F32),
        compiler_params=_cparams(("arbitrary", "arbitrary", "arbitrary")),
        name="conv_silu",
    )(x, x, x, w, bias.reshape(1, c))


def _chunk_masks(fwd):
    ii = lax.broadcasted_iota(I32, (CHUNK, CHUNK), 0)
    jj = lax.broadcasted_iota(I32, (CHUNK, CHUNK), 1)
    incl = (ii >= jj) if fwd else (ii <= jj)
    strict = (ii > jj) if fwd else (ii < jj)
    return incl, strict


def _cumsum_chunk(incl, v):
    return jnp.dot(incl.astype(F32), v, precision=HIGHEST, preferred_element_type=F32)


def _softplus(x):
    return jnp.maximum(x, 0.0) + jnp.log1p(jnp.exp(-jnp.abs(x)))


def _ssd_chunk(dirs, bias, a, spread_ref, diag_ref):
    units = []
    for d, (x_ref, b_ref, c_ref, dt_ref, s_ref, y_ref, col0, fwd) in enumerate(dirs):
        incl, _ = _chunk_masks(fwd)
        bm_t = b_ref[0].T.astype(BF16)
        cm = c_ref[0].astype(BF16)
        dt = _softplus(dt_ref[0] + bias)
        acs = _cumsum_chunk(incl, dt * a)
        acs_t = acs.T
        tot = acs[CHUNK - 1:CHUNK] if fwd else acs[0:1]
        cols = jnp.concatenate([dt, jnp.exp(tot - acs), jnp.exp(acs),
                                jnp.broadcast_to(jnp.exp(tot), (SUBLANES, LANES))], axis=0)
        units.append(dict(
            x_ref=x_ref, y_ref=y_ref, s_ref=s_ref, bm_t=bm_t, cm=cm, cols=cols, spread=spread_ref[d],
            cb=_dot(cm, bm_t),
            lms=[jnp.exp(jnp.where(incl, acs[:, col0 + k:col0 + k + 1] - acs_t[col0 + k:col0 + k + 1, :], -jnp.inf))
                 for k in range(SSD_HPG)]))
    wide = []
    for u in units:
        h1 = u["cols"].astype(BF16)
        r1 = u["cols"] - h1.astype(F32)
        h2 = r1.astype(BF16)
        h3 = (r1 - h2.astype(F32)).astype(BF16)
        wide.append(_dot(h1, u["spread"]) + _dot(h2, u["spread"]) + _dot(h3, u["spread"]))
    q = CHUNK
    xdt = [u["x_ref"][0] * w[0:q] for u, w in zip(units, wide)]
    m_cat = [jnp.concatenate([u["cb"] * lm for lm in u["lms"]], axis=1).astype(BF16) for u in units]
    y_in = [_dot(m, jnp.tile(xd.astype(BF16), (SSD_HPG, 1)) * diag_ref[...]) for m, xd in zip(m_cat, xdt)]
    ss = [u["s_ref"][...] for u in units]
    y_x = [_dot(u["cm"], s.astype(BF16)) * w[2 * q:3 * q] for u, s, w in zip(units, ss, wide)]
    for u, yi, yx in zip(units, y_in, y_x):
        u["y_ref"][0] = yi + yx
    new = [s * w[3 * q:3 * q + 1] + _dot(u["bm_t"], (xd * w[q:2 * q]).astype(BF16))
           for u, s, w, xd in zip(units, ss, wide, xdt)]
    for u, s in zip(units, new):
        u["s_ref"][...] = s


def _load_states(states, s0_ref):
    for k, s in enumerate(states):
        s[...] = jnp.zeros_like(s) if s0_ref is None else s0_ref[0, k]


def _store_states(out_ref, states):
    for k, s in enumerate(states):
        out_ref[0, k] = s[...]


def _ssd_body(*refs, has_state):
    (xf, bf, cf, dtf, xb, bb, cbk, dtb, bias_ref, alog_ref, spread_ref, diag_ref), refs = refs[:12], refs[12:]
    s0f = s0b = None
    if has_state:
        (s0f, s0b), refs = refs[:2], refs[2:]
    yf, yb, sfo, sbo, sf, sb = refs
    c = pl.program_id(2)
    p = SSD_HEAD_DIM

    @pl.when(c == 0)
    def _():
        for s, s0 in ((sf, s0f), (sb, s0b)):
            for k in range(SSD_HPG):
                s[:, k * p:(k + 1) * p] = jnp.zeros((SSD_STATE, p), F32) if s0 is None else s0[0, k].T

    bias = bias_ref[0]
    a = -jnp.exp(alog_ref[0])
    _ssd_chunk([(xf, bf, cf, dtf, sf, yf, 0, True), (xb, bb, cbk, dtb, sb, yb, SSD_HPG, False)],
               bias, a, spread_ref, diag_ref)

    @pl.when(c == pl.num_programs(2) - 1)
    def _():
        for s, out in ((sf, sfo), (sb, sbo)):
            for k in range(SSD_HPG):
                out[0, k] = s[:, k * p:(k + 1) * p].T


def _ssd_scan(xbc, dt, bias, alog, s0f=None, s0b=None):
    b, l, _ = xbc.shape
    nc = l // CHUNK
    gw = SSD_HPG * SSD_HEAD_DIM
    boff = SSD_INNER // SSD_STATE
    coff = boff + SSD_GROUPS
    has_state = s0f is not None

    def seq_specs(ci):
        return [
            pl.BlockSpec((1, CHUNK, gw), lambda bi, g, c: (bi, ci(c), g)),
            pl.BlockSpec((1, CHUNK, SSD_STATE), lambda bi, g, c: (bi, ci(c), boff + g)),
            pl.BlockSpec((1, CHUNK, SSD_STATE), lambda bi, g, c: (bi, ci(c), coff + g)),
            pl.BlockSpec((1, CHUNK, LANES), lambda bi, g, c: (bi, ci(c), g)),
        ]

    fw = lambda c: c
    bw = lambda c: nc - 1 - c
    st_spec = pl.BlockSpec((1, SSD_HPG, SSD_HEAD_DIM, SSD_STATE), lambda bi, g, c: (bi, g, 0, 0))
    par_spec = pl.BlockSpec((1, 1, LANES), lambda bi, g, c: (g, 0, 0))
    head_of_lane = jnp.arange(gw) // SSD_HEAD_DIM
    lane_col = jnp.arange(LANES)[:, None]
    spread = jnp.stack([lane_col == head_of_lane[None, :], lane_col == head_of_lane[None, :] + SSD_HPG]).astype(BF16)
    diag = (jnp.arange(SSD_HPG * CHUNK)[:, None] // CHUNK == head_of_lane[None, :]).astype(BF16)
    in_specs = seq_specs(fw) + seq_specs(bw) + [par_spec, par_spec, _const_spec(spread.shape), _const_spec(diag.shape)]
    args = [xbc, xbc, xbc, dt, xbc, xbc, xbc, dt, bias, alog, spread, diag]
    if has_state:
        in_specs += [st_spec, st_spec]
        args += [s0f, s0b]
    st_shape = jax.ShapeDtypeStruct((b, SSD_HEADS, SSD_HEAD_DIM, SSD_STATE), F32)
    y_shape = jax.ShapeDtypeStruct((b, l, SSD_INNER), F32)
    return pl.pallas_call(
        functools.partial(_ssd_body, has_state=has_state),
        grid=(b, SSD_GROUPS, nc),
        in_specs=in_specs,
        out_specs=[
            pl.BlockSpec((1, CHUNK, gw), lambda bi, g, c: (bi, c, g)),
            pl.BlockSpec((1, CHUNK, gw), lambda bi, g, c: (bi, nc - 1 - c, g)),
            st_spec, st_spec,
        ],
        out_shape=[y_shape, y_shape, st_shape, st_shape],
        scratch_shapes=[pltpu.VMEM((SSD_STATE, gw), F32)] * 2,
        compiler_params=_cparams(("arbitrary", "arbitrary", "arbitrary")),
        name="ssd_scan",
    )(*args)


def _ssd_out_body(yf_ref, yb_ref, xs_ref, z_ref, d_ref, ng_ref, w_ref, x_ref, gate_ref, o_ref):
    y = yf_ref[0] + yb_ref[0] + xs_ref[0] * d_ref[...]
    y = y * _silu(z_ref[0])
    y = y * lax.rsqrt(jnp.mean(y * y, axis=-1, keepdims=True) + EPS) * ng_ref[...]
    o_ref[0] = x_ref[0] + gate_ref[0] * _dot(y.astype(BF16), w_ref[...])


def _ssd_out(yf, yb, xbc, z, d_skip, norm_g, w_out, x, gate):
    b, l, d = x.shape
    tl = min(l, 256)
    inner = SSD_INNER
    tok = lambda n: pl.BlockSpec((1, tl, n), lambda i, j: (i, j, 0))
    return pl.pallas_call(
        _ssd_out_body,
        grid=(b, l // tl),
        in_specs=[tok(inner), tok(inner), tok(inner), tok(inner),
                  _const_spec((1, inner)), _const_spec((1, inner)), _const_spec(w_out.shape),
                  tok(d), _mod_spec(gate.shape[0])],
        out_specs=tok(d),
        out_shape=jax.ShapeDtypeStruct(x.shape, F32),
        compiler_params=_cparams(("arbitrary", "arbitrary")),
        name="ssd_out",
    )(yf, yb, xbc, z, d_skip, norm_g, w_out, x, gate)


def _mm(a, b):
    return _dot(a.astype(BF16), b.astype(BF16))


def _mm_nt(a, b):
    return _dot_nt(a.astype(BF16), b.astype(BF16))


INV_BASE_LOG2 = 3


def _unit_tri_inverses(ms):
    ii = lax.broadcasted_iota(I32, (CHUNK, CHUNK), 0)
    jj = lax.broadcasted_iota(I32, (CHUNK, CHUNK), 1)
    eye = jnp.where(ii == jj, 1.0, 0.0)
    base = (ii >> INV_BASE_LOG2) == (jj >> INV_BASE_LOG2)
    pws = [jnp.where(base, m, 0.0) for m in ms]
    xs = [eye - d for d in pws]
    for _ in range(INV_BASE_LOG2 - 1):
        pws = [_mm(pw, pw) for pw in pws]
        xs = [x + _mm(x, pw) for x, pw in zip(xs, pws)]
    for lg in range(INV_BASE_LOG2, int(math.log2(CHUNK))):
        off_diag = ((ii >> (lg + 1)) == (jj >> (lg + 1))) & ((ii >> lg) != (jj >> lg))
        cxs = [_mm(jnp.where(off_diag, m, 0.0), x) for m, x in zip(ms, xs)]
        xs = [x - _mm(x, cx) for x, cx in zip(xs, cxs)]
    return xs


def _gdn_chunk(dirs, bias, neg_a):
    dk, dv = GDN_DK, GDN_DV
    units = []
    for q_ref, k_ref, v_ref, ab_ref, states, o_ref, off, fwd in dirs:
        incl, strict = _chunk_masks(fwd)
        q = q_ref[0]
        k = k_ref[0]
        v = v_ref[0]
        ab = ab_ref[0]
        beta_all = jax.nn.sigmoid(ab)
        gc = _cumsum_chunk(incl, neg_a * _softplus(ab + bias))
        gc_t = gc.T
        tot = gc[CHUNK - 1:CHUNK] if fwd else gc[0:1]
        e_gc = jnp.exp(gc)
        e_rest = jnp.exp(tot - gc)
        g_last = jnp.exp(tot)
        for h in range(GDN_HPG):
            cb = off + h
            ca = 2 * GDN_HPG + off + h
            kh = k[:, h * dk:(h + 1) * dk]
            beta = beta_all[:, cb:cb + 1]
            units.append(dict(
                incl=incl, strict=strict, kh=kh, kb=kh * beta,
                qh=q[:, h * dk:(h + 1) * dk], vb=v[:, h * dv:(h + 1) * dv] * beta,
                decay=jnp.exp(jnp.where(incl, gc[:, ca:ca + 1] - gc_t[ca:ca + 1, :], -jnp.inf)),
                e_gc=e_gc[:, ca:ca + 1], e_rest=e_rest[:, ca:ca + 1], g_last=g_last[:, ca:ca + 1],
                s_ref=states[h], o_ref=o_ref, cols=slice(h * dv, (h + 1) * dv)))
    ms = [jnp.where(u["strict"], _mm_nt(u["kb"], u["kh"]) * u["decay"], 0.0) for u in units]
    ts = _unit_tri_inverses(ms)
    us = [_mm(t, u["vb"]) for t, u in zip(ts, units)]
    ws = [_mm(t, u["kb"] * u["e_gc"]) for t, u in zip(ts, units)]
    aqk = [jnp.where(u["incl"], _mm_nt(u["qh"], u["kh"]) * u["decay"], 0.0) for u in units]
    ss = [u["s_ref"][...] for u in units]
    vn = [x - _mm(w, s) for x, w, s in zip(us, ws, ss)]
    os_ = [_mm(u["qh"] * u["e_gc"], s) + _mm(a, v) for u, s, a, v in zip(units, ss, aqk, vn)]
    for u, o in zip(units, os_):
        u["o_ref"][0, :, u["cols"]] = o
    new = [s * u["g_last"] + _mm((u["kh"] * u["e_rest"]).T, v) for u, s, v in zip(units, ss, vn)]
    for u, s in zip(units, new):
        u["s_ref"][...] = s


def _gdn_body(*refs, has_state):
    (qf, kf, vf, abf, qb, kb, vb, abb, bias_ref, alog_ref), refs = refs[:10], refs[10:]
    s0f = s0b = None
    if has_state:
        (s0f, s0b), refs = refs[:2], refs[2:]
    (of, ob, sfo, sbo), states = refs[:4], refs[4:]
    sf, sb = states[:GDN_HPG], states[GDN_HPG:]
    c = pl.program_id(2)

    @pl.when(c == 0)
    def _():
        _load_states(sf, s0f)
        _load_states(sb, s0b)

    bias = bias_ref[0]
    neg_a = -jnp.exp(alog_ref[0])
    _gdn_chunk([(qf, kf, vf, abf, sf, of, 0, True), (qb, kb, vb, abb, sb, ob, GDN_HPG, False)], bias, neg_a)

    @pl.when(c == pl.num_programs(2) - 1)
    def _():
        _store_states(sfo, sf)
        _store_states(sbo, sb)


def _gdn_scan(qkv, ab, bias, alog, s0f=None, s0b=None):
    b, l, _ = qkv.shape
    nc = l // CHUNK
    qw = GDN_HPG * GDN_DK
    vw = GDN_HPG * GDN_DV
    koff = GDN_QK // qw
    voff = 2 * GDN_QK // vw
    has_state = s0f is not None

    def seq_specs(ci):
        return [
            pl.BlockSpec((1, CHUNK, qw), lambda bi, g, c: (bi, ci(c), g)),
            pl.BlockSpec((1, CHUNK, qw), lambda bi, g, c: (bi, ci(c), koff + g)),
            pl.BlockSpec((1, CHUNK, vw), lambda bi, g, c: (bi, ci(c), voff + g)),
            pl.BlockSpec((1, CHUNK, LANES), lambda bi, g, c: (bi, ci(c), g)),
        ]

    fw = lambda c: c
    bw = lambda c: nc - 1 - c
    st_spec = pl.BlockSpec((1, GDN_HPG, GDN_DK, GDN_DV), lambda bi, g, c: (bi, g, 0, 0))
    par_spec = pl.BlockSpec((1, 1, LANES), lambda bi, g, c: (g, 0, 0))
    in_specs = seq_specs(fw) + seq_specs(bw) + [par_spec, par_spec]
    args = [qkv, qkv, qkv, ab, qkv, qkv, qkv, ab, bias, alog]
    if has_state:
        in_specs += [st_spec, st_spec]
        args += [s0f, s0b]
    st_shape = jax.ShapeDtypeStruct((b, GDN_HEADS, GDN_DK, GDN_DV), F32)
    o_shape = jax.ShapeDtypeStruct((b, l, GDN_VW), F32)
    return pl.pallas_call(
        functools.partial(_gdn_body, has_state=has_state),
        grid=(b, GDN_GROUPS, nc),
        in_specs=in_specs,
        out_specs=[
            pl.BlockSpec((1, CHUNK, vw), lambda bi, g, c: (bi, c, g)),
            pl.BlockSpec((1, CHUNK, vw), lambda bi, g, c: (bi, nc - 1 - c, g)),
            st_spec, st_spec,
        ],
        out_shape=[o_shape, o_shape, st_shape, st_shape],
        scratch_shapes=[pltpu.VMEM((GDN_DK, GDN_DV), F32)] * (2 * GDN_HPG),
        compiler_params=_cparams(("arbitrary", "arbitrary", "arbitrary")),
        name="gdn_scan",
    )(*args)


def _gdn_out_body(of_ref, ob_ref, z_ref, ng_ref, w_ref, x_ref, gate_ref, o_ref, y_ref):
    dv = GDN_DV
    for h in range(GDN_HEADS):
        sl = slice(h * dv, (h + 1) * dv)
        o = of_ref[0, :, sl] + ob_ref[0, :, sl]
        o = o * lax.rsqrt(jnp.mean(o * o, axis=-1, keepdims=True) + EPS) * ng_ref[...]
        y_ref[:, sl] = (o * _silu(z_ref[0, :, sl])).astype(BF16)
    o_ref[0] = x_ref[0] + gate_ref[0] * _dot(y_ref[...], w_ref[...])


def _gdn_out(of, ob, z, norm_g, w_out, x, gate):
    b, l, d = x.shape
    tl = min(l, 256)
    tok = lambda n: pl.BlockSpec((1, tl, n), lambda i, j: (i, j, 0))
    return pl.pallas_call(
        _gdn_out_body,
        grid=(b, l // tl),
        in_specs=[tok(GDN_VW), tok(GDN_VW), tok(GDN_VW),
                  _const_spec((1, GDN_DV)), _const_spec(w_out.shape),
                  tok(d), _mod_spec(gate.shape[0])],
        out_specs=tok(d),
        out_shape=jax.ShapeDtypeStruct(x.shape, F32),
        scratch_shapes=[pltpu.VMEM((tl, GDN_VW), BF16)],
        compiler_params=_cparams(("arbitrary", "arbitrary")),
        name="gdn_out",
    )(of, ob, z, norm_g, w_out, x, gate)


def _topk_rows(s, k, ids=None):
    if ids is None:
        ids = lax.broadcasted_iota(I32, s.shape, 0)
        fill = s.shape[0] - 1
    else:
        fill = N_EXPERTS
    vals, idxs = [], []
    for _ in range(k):
        m = jnp.max(s, axis=0, keepdims=True)
        idx = jnp.min(jnp.where(s == m, ids, fill), axis=0, keepdims=True)
        vals.append(m)
        idxs.append(idx)
        s = jnp.where(ids == idx, -jnp.inf, s)
    return jnp.concatenate(vals, axis=0), jnp.concatenate(idxs, axis=0)


def _pair_candidates(t):
    k = PEER_TOPK
    r = lax.broadcasted_iota(I32, (k + 7 * SUBLANES + SUBLANES, t), 0)
    mid = r - k
    a_mid = (mid >> 3) + 1
    b_mid = mid & (SUBLANES - 1)
    tail = r - (k + 7 * SUBLANES)
    ids = jnp.where(r < k, r, jnp.where(tail < 0, a_mid * k + b_mid, (tail + SUBLANES) * k))
    valid = (r < k) | (tail >= 0) | ((a_mid + 1) * (b_mid + 1) <= k)
    return ids, valid


def _pick_rows(sel, table):
    out = jnp.zeros(sel.shape, table.dtype)
    for a in range(table.shape[0]):
        out = jnp.where(sel == a, table[a:a + 1], out)
    return out


def _peer_select_body(x_ref, g_ref, sh_ref, sc_ref, wqh_ref, wql_ref, kh_ref, kl_ref,
                      h_ref, eid_ref, gate_ref):
    h = _adaln(x_ref[0], g_ref[...], sh_ref[0], sc_ref[0])
    for m in range(ROW_VREGS):
        h_ref[0, pl.ds(m, h.shape[0], stride=ROW_VREGS), :] = h[:, m * LANES:(m + 1) * LANES]
    hh, hl = _split_bf16(h)
    q = _dot(hh, wqh_ref[...]) + _dot(hh, wql_ref[...]) + _dot(hl, wqh_ref[...])
    eids, gates = [], []
    cand_ids, cand_ok = _pair_candidates(h.shape[0])
    for hd in range(PEER_HEADS):
        sv, si = [], []
        for s in range(2):
            c0 = hd * PEER_DKEY + s * PEER_HALF
            qh, ql = _split_bf16(q[:, c0:c0 + PEER_HALF])
            kh = kh_ref[s, hd]
            st = _dot_nt(kh, qh) + _dot_nt(kh, ql) + _dot_nt(kl_ref[s, hd], qh)
            v, i = _topk_rows(st, PEER_TOPK)
            sv.append(v)
            si.append(i)
        comb = jnp.concatenate(
            [sv[0][0:1] + sv[1]]
            + [sv[0][a:a + 1] + sv[1][0:SUBLANES] for a in range(1, SUBLANES)]
            + [sv[0][SUBLANES:] + sv[1][0:1]], axis=0)
        cv, ci = _topk_rows(jnp.where(cand_ok, comb, -jnp.inf), PEER_TOPK, cand_ids)
        shift = int(math.log2(PEER_TOPK))
        i1 = _pick_rows(ci >> shift, si[0])
        i2 = _pick_rows(ci & (PEER_TOPK - 1), si[1])
        eids.append(i1 * N_KEYS + i2)
        e = jnp.exp(cv - cv[0:1])
        gates.append(e / jnp.sum(e, axis=0, keepdims=True))
    eid_ref[0] = jnp.concatenate(eids, axis=0).astype(F32).T.astype(I32)
    gate_ref[0] = jnp.concatenate(gates, axis=0).T


def _peer_select(x, g, shift, scale, wq_hi, wq_lo, k_hi, k_lo):
    b, l, d = x.shape
    tl = min(l, 256)
    bm = shift.shape[0]
    tok = lambda n: pl.BlockSpec((1, tl, n), lambda i, j: (i, j, 0))
    return pl.pallas_call(
        _peer_select_body,
        grid=(b, l // tl),
        in_specs=[tok(d), _const_spec((1, d)), _mod_spec(bm), _mod_spec(bm),
                  _const_spec(wq_hi.shape), _const_spec(wq_lo.shape),
                  _const_spec(k_hi.shape), _const_spec(k_lo.shape)],
        out_specs=[pl.BlockSpec((1, tl * ROW_VREGS, LANES), lambda i, j: (i, j, 0)),
                   tok(PEER_SLOTS), tok(PEER_SLOTS)],
        out_shape=[jax.ShapeDtypeStruct((b, l * ROW_VREGS, LANES), F32),
                   jax.ShapeDtypeStruct((b, l, PEER_SLOTS), I32),
                   jax.ShapeDtypeStruct((b, l, PEER_SLOTS), F32)],
        compiler_params=_cparams(("arbitrary", "arbitrary")),
        name="peer_select",
    )(x, g.reshape(1, d), shift, scale, wq_hi, wq_lo, k_hi, k_lo)


def _pack_body(t_ref, o_ref):
    t = t_ref[...]
    half = t.shape[1] // 2
    lo = lax.bitcast_convert_type(t[:, :half].astype(BF16).astype(F32), jnp.uint32)
    hi = lax.bitcast_convert_type(t[:, half:].astype(BF16).astype(F32), jnp.uint32)
    word = (hi & jnp.uint32(0xFFFF0000)) | (lo >> 16)
    o_ref[...] = lax.bitcast_convert_type(word, I32)


def _pack_table(tab):
    e, d = tab.shape
    r = 512
    out = pl.pallas_call(
        _pack_body,
        grid=(e // r,),
        in_specs=[pl.BlockSpec((r, d), lambda i: (i, 0))],
        out_specs=pl.BlockSpec((r, d // 2), lambda i: (i, 0)),
        out_shape=jax.ShapeDtypeStruct((e, d // 2), I32),
        compiler_params=_cparams(("arbitrary",)),
        name="pack_table",
    )(tab)
    return out.reshape(e, HALF_ROWS, LANES)


def _unpack_words(w):
    lo = lax.bitcast_convert_type(w << 16, F32)
    hi = lax.bitcast_convert_type(w & jnp.int32(-65536), F32)
    return lo, hi


PEER_LAG = 2


def _peer_u_body(eid_ref, hr_ref, gate_ref, tab_ref, act_ref, ps_a, ps_b, acc_ref, *, tb):
    ones = jnp.ones((LANES, LANES), BF16)
    lane = lax.broadcasted_iota(I32, (PEER_SLOTS, tb), 1)

    def gather(t, ps_ref):
        xr = hr_ref[0, pl.ds(pl.multiple_of(t * ROW_VREGS, ROW_VREGS), ROW_VREGS), :]
        x_lo = xr[0:HALF_ROWS]
        x_hi = xr[HALF_ROWS:ROW_VREGS]
        for j in range(PEER_SLOTS):
            lo, hi = _unpack_words(tab_ref[eid_ref[0, t, j]])
            ps_ref[j * HALF_ROWS:(j + 1) * HALF_ROWS, :] = lo * x_lo + hi * x_hi

    def reduce(ps_ref, tok):
        part = ps_ref[pl.ds(0, PEER_SLOTS, stride=HALF_ROWS), :]
        for s in range(1, HALF_ROWS):
            part = part + ps_ref[pl.ds(s, PEER_SLOTS, stride=HALF_ROWS), :]
        p_hi, p_lo = _split_bf16(part)
        sums = _dot(p_hi, ones) + _dot(p_lo, ones)
        acc_ref[...] = jnp.where(lane == tok, sums, acc_ref[...])

    ps_a[...] = jnp.zeros_like(ps_a)
    ps_b[...] = jnp.zeros_like(ps_b)
    acc_ref[...] = jnp.zeros_like(acc_ref)

    def pair(i, carry):
        t = i * PEER_LAG
        reduce(ps_a, t - PEER_LAG)
        reduce(ps_b, t - PEER_LAG + 1)
        gather(t, ps_a)
        gather(t + 1, ps_b)
        return carry

    lax.fori_loop(0, tb // PEER_LAG, pair, 0)
    reduce(ps_a, tb - PEER_LAG)
    reduce(ps_b, tb - PEER_LAG + 1)
    a = acc_ref[...].T
    act_ref[0] = 0.5 * a * (1.0 + lax.erf(a * (2.0 ** -0.5))) * gate_ref[0]


def _peer_v_body(eid_ref, act_ref, x_ref, gate_ref, eye_ref, tab_ref, o_ref, wb_a, wb_b, res_ref, *, tb):
    ones = jnp.ones((PEER_SLOTS, LANES), BF16)

    def spread(t, wb_ref):
        a = act_ref[0, pl.ds(t, 1), :]
        diag = jnp.where(eye_ref[...] != 0.0, a, 0.0)
        wb_ref[...] = _dot(diag.astype(BF16), ones)

    def combine(t, wb_ref):
        n_acc = 2
        acc_lo = [jnp.zeros((HALF_ROWS, LANES), F32) for _ in range(n_acc)]
        acc_hi = [jnp.zeros((HALF_ROWS, LANES), F32) for _ in range(n_acc)]
        for j in range(PEER_SLOTS):
            lo, hi = _unpack_words(tab_ref[eid_ref[0, t, j]])
            wj = wb_ref[j:j + 1, :]
            acc_lo[j % n_acc] = acc_lo[j % n_acc] + lo * wj
            acc_hi[j % n_acc] = acc_hi[j % n_acc] + hi * wj
        rows = pl.ds(pl.multiple_of(t * ROW_VREGS, ROW_VREGS), ROW_VREGS)
        res_ref[rows, :] = jnp.concatenate([acc_lo[0] + acc_lo[1], acc_hi[0] + acc_hi[1]], axis=0)

    spread(0, wb_a)

    def pair(i, carry):
        t = i * PEER_LAG
        spread(t + 1, wb_b)
        combine(t, wb_a)
        spread(jnp.minimum(t + PEER_LAG, tb - 1), wb_a)
        combine(t + 1, wb_b)
        return carry

    lax.fori_loop(0, tb // PEER_LAG, pair, 0)
    for m in range(ROW_VREGS):
        cols = slice(m * LANES, (m + 1) * LANES)
        o_ref[0, :, cols] = x_ref[0, :, cols] + gate_ref[0, :, cols] * res_ref[pl.ds(m, tb, stride=ROW_VREGS), :]


def _peer_retrieve(x, hr, eid, gate_w, mod_gate, tab_u, tab_v):
    b, l, d = x.shape
    tb = 128
    tok = pl.BlockSpec((1, tb, PEER_SLOTS), lambda i, j: (i, j, 0))
    eid_spec = pl.BlockSpec((1, tb, PEER_SLOTS), lambda i, j: (i, j, 0), memory_space=pltpu.SMEM)
    row_spec = pl.BlockSpec((1, tb * ROW_VREGS, LANES), lambda i, j: (i, j, 0))
    x_spec = pl.BlockSpec((1, tb, d), lambda i, j: (i, j, 0))
    tab_spec = pl.BlockSpec(memory_space=pltpu.VMEM)
    assert tb == LANES
    eye = jnp.eye(PEER_SLOTS, dtype=F32)
    part = pltpu.VMEM((PEER_SLOTS * HALF_ROWS, LANES), F32)
    act = pl.pallas_call(
        functools.partial(_peer_u_body, tb=tb),
        grid=(b, l // tb),
        in_specs=[eid_spec, row_spec, tok, tab_spec],
        out_specs=tok,
        out_shape=jax.ShapeDtypeStruct((b, l, PEER_SLOTS), F32),
        scratch_shapes=[part, part, pltpu.VMEM((PEER_SLOTS, tb), F32)],
        compiler_params=_cparams(("arbitrary", "arbitrary")),
        name="peer_u",
    )(eid, hr, gate_w, tab_u)
    spread = pltpu.VMEM((PEER_SLOTS, LANES), F32)
    return pl.pallas_call(
        functools.partial(_peer_v_body, tb=tb),
        grid=(b, l // tb),
        in_specs=[eid_spec, tok, x_spec, _mod_spec(mod_gate.shape[0]), _const_spec(eye.shape), tab_spec],
        out_specs=x_spec,
        out_shape=jax.ShapeDtypeStruct(x.shape, F32),
        scratch_shapes=[spread, spread, pltpu.VMEM((tb * ROW_VREGS, LANES), F32)],
        compiler_params=_cparams(("arbitrary", "arbitrary")),
        name="peer_v",
    )(eid, act, x, mod_gate, eye, tab_v)


def _final_norm_body(x_ref, g_ref, o_ref):
    x = x_ref[0]
    o_ref[0] = x * lax.rsqrt(jnp.mean(x * x, axis=-1, keepdims=True) + EPS) * g_ref[...]


def _final_norm(x, g):
    b, l, d = x.shape
    tl = min(l, 512)
    return pl.pallas_call(
        _final_norm_body,
        grid=(b, l // tl),
        in_specs=[pl.BlockSpec((1, tl, d), lambda i, j: (i, j, 0)), _const_spec((1, d))],
        out_specs=pl.BlockSpec((1, tl, d), lambda i, j: (i, j, 0)),
        out_shape=jax.ShapeDtypeStruct(x.shape, F32),
        compiler_params=_cparams(("arbitrary", "arbitrary")),
        name="final_norm",
    )(x, g.reshape(1, d))


def _group_blocked(cols, per_group, groups):
    d = cols[0].shape[0]
    blocks = []
    for g in range(groups):
        parts = [c[:, g * per_group:(g + 1) * per_group] for c in cols]
        used = per_group * len(cols)
        blocks.append(jnp.concatenate(parts + [jnp.zeros((d, LANES - used), cols[0].dtype)], axis=1))
    return jnp.concatenate(blocks, axis=1)


def _group_blocked_vec(vecs, per_group, groups):
    rows = []
    for g in range(groups):
        parts = [jnp.zeros((per_group,), F32) if v is None else v[g * per_group:(g + 1) * per_group]
                 for v in vecs]
        used = per_group * len(vecs)
        rows.append(jnp.concatenate(parts + [jnp.zeros((LANES - used,), F32)]))
    return jnp.stack(rows).reshape(groups, 1, LANES)


def _ssd_params(w_in, conv_w, conv_b, dt_bias, a_log, d_skip, norm_g, w_out):
    dt0 = SSD_INNER + SSD_CONV_DIM
    w_dt = _group_blocked([w_in[:, dt0:dt0 + SSD_HEADS], w_in[:, dt0 + SSD_HEADS:]], SSD_HPG, SSD_GROUPS)
    return dict(
        ws=[w_in[:, :SSD_INNER].astype(BF16), w_in[:, SSD_INNER:dt0].astype(BF16), w_dt.astype(BF16)],
        conv_w=conv_w, conv_b=conv_b,
        bias=_group_blocked_vec([dt_bias[0], dt_bias[1]], SSD_HPG, SSD_GROUPS),
        alog=_group_blocked_vec([a_log[0], a_log[1]], SSD_HPG, SSD_GROUPS),
        d_skip=jnp.repeat(d_skip, SSD_HEAD_DIM).reshape(1, SSD_INNER),
        norm_g=norm_g.reshape(1, SSD_INNER),
        w_out=w_out.astype(BF16),
    )


def _gdn_params(w_in, conv_w, dt_bias, a_log, norm_g, w_out):
    ab0 = GDN_CONV_DIM + GDN_VW
    h = GDN_HEADS
    ab = [w_in[:, ab0 + i * h:ab0 + (i + 1) * h] for i in range(4)]
    return dict(
        ws=[w_in[:, :GDN_CONV_DIM].astype(BF16), w_in[:, GDN_CONV_DIM:ab0].astype(BF16),
            _group_blocked(ab, GDN_HPG, GDN_GROUPS).astype(BF16)],
        conv_w=conv_w,
        bias=_group_blocked_vec([None, None, dt_bias[0], dt_bias[1]], GDN_HPG, GDN_GROUPS),
        alog=_group_blocked_vec([None, None, a_log[0], a_log[1]], GDN_HPG, GDN_GROUPS),
        norm_g=norm_g.reshape(1, GDN_DV),
        w_out=w_out.astype(BF16),
    )


def _peer_params(w_q, keys, u_tab, v_tab):
    wq_hi, wq_lo = _split_bf16(w_q)
    k_hi, k_lo = _split_bf16(keys)
    return dict(wq_hi=wq_hi, wq_lo=wq_lo, k_hi=k_hi, k_lo=k_lo,
                tab_u=_pack_table(u_tab), tab_v=_pack_table(v_tab))


def _ssd_layer(x, m, g, p, s0f=None, s0b=None):
    z, xbc, dt = _norm_proj(x, g, m[0], m[1], p["ws"])
    xbc = _conv_silu(xbc, p["conv_w"], p["conv_b"])
    yf, yb, sf, sb = _ssd_scan(xbc, dt, p["bias"], p["alog"], s0f, s0b)
    x = _ssd_out(yf, yb, xbc, z, p["d_skip"], p["norm_g"], p["w_out"], x, m[2])
    return x, sf, sb


def _gdn_layer(x, m, g, p, s0f=None, s0b=None):
    qkv, z, ab = _norm_proj(x, g, m[0], m[1], p["ws"])
    qkv = _conv_silu(qkv, p["conv_w"], jnp.zeros((GDN_CONV_DIM,), F32),
                     n_l2=2 * GDN_QK // 512, n_qscale=GDN_QK // 512)
    of, ob, sf, sb = _gdn_scan(qkv, ab, p["bias"], p["alog"], s0f, s0b)
    x = _gdn_out(of, ob, z, p["norm_g"], p["w_out"], x, m[2])
    return x, sf, sb


def _peer_layer(x, m, g, p):
    h, eid, gate_w = _peer_select(x, g, m[3], m[4], p["wq_hi"], p["wq_lo"], p["k_hi"], p["k_lo"])
    return _peer_retrieve(x, h, eid, gate_w, m[5], p["tab_u"], p["tab_v"])


def kernel(x_prompt, x_sample, c, state_ssd_fwd, state_ssd_bwd, state_gdn_fwd, state_gdn_bwd, c_ctx, w_mod, b_mod, norm_mix_g, norm_ffn_g, ssd_w_in, ssd_conv_w, ssd_conv_b, ssd_dt_bias, ssd_a_log, ssd_d, ssd_norm_g, ssd_w_out, gdn_w_in, gdn_conv_w, gdn_dt_bias, gdn_a_log, gdn_norm_g, gdn_w_out, peer_w_q, peer_keys, peer_u, peer_v, final_norm_g):
    d = D_MODEL
    nb = c.shape[0]
    rows = 2 * SUBLANES
    c_all = jnp.zeros((rows, d), F32).at[0].set(c_ctx).at[1:1 + nb].set(c)
    mod = _modulation(c_all, w_mod, b_mod)
    xp = x_prompt
    xs = _add_pos(x_sample, _grid_pos_embed(x_sample.shape[1]))
    ssd_f, ssd_b, gdn_f, gdn_b = [], [], [], []
    for i in range(DEPTH):
        mp = [mod[i, 0:1, k * d:(k + 1) * d].reshape(1, 1, d) for k in range(6)]
        ms = [mod[i, 1:1 + nb, k * d:(k + 1) * d].reshape(nb, 1, d) for k in range(6)]
        j = i // 2
        if i % 2 == 0:
            p = _ssd_params(ssd_w_in[j], ssd_conv_w[j], ssd_conv_b[j], ssd_dt_bias[j], ssd_a_log[j],
                            ssd_d[j], ssd_norm_g[j], ssd_w_out[j])
            xp, sf, sb = _ssd_layer(xp, mp, norm_mix_g[i], p)
            xs, _, _ = _ssd_layer(xs, ms, norm_mix_g[i], p, state_ssd_fwd[:, j], state_ssd_bwd[:, j])
            ssd_f.append(sf)
            ssd_b.append(sb)
        else:
            p = _gdn_params(gdn_w_in[j], gdn_conv_w[j], gdn_dt_bias[j], gdn_a_log[j], gdn_norm_g[j], gdn_w_out[j])
            xp, sf, sb = _gdn_layer(xp, mp, norm_mix_g[i], p)
            xs, _, _ = _gdn_layer(xs, ms, norm_mix_g[i], p, state_gdn_fwd[:, j], state_gdn_bwd[:, j])
            gdn_f.append(sf)
            gdn_b.append(sb)
        pp = _peer_params(peer_w_q[i], peer_keys[i], peer_u[i], peer_v[i])
        xp = _peer_layer(xp, mp, norm_ffn_g[i], pp)
        xs = _peer_layer(xs, ms, norm_ffn_g[i], pp)
    y_prompt = _final_norm(xp, final_norm_g)
    y_sample = _final_norm(xs, final_norm_g)
    return (y_prompt, y_sample, jnp.stack(ssd_f, axis=1), jnp.stack(ssd_b, axis=1),
            jnp.stack(gdn_f, axis=1), jnp.stack(gdn_b, axis=1))
```

```python
import functools
import math

import jax
import jax.numpy as jnp
from jax import lax
from jax.experimental import pallas as pl
from jax.experimental.pallas import tpu as pltpu

F32 = jnp.float32
BF16 = jnp.bfloat16
I32 = jnp.int32
HIGHEST = lax.Precision.HIGHEST

D_MODEL = 1024
DEPTH = 2
GRID_W = 64
CHUNK = 64
EPS = 1e-6
SSD_INNER = 2 * D_MODEL
SSD_HEAD_DIM = 64
SSD_HEADS = SSD_INNER // SSD_HEAD_DIM
SSD_GROUPS = 4
SSD_HPG = SSD_HEADS // SSD_GROUPS
SSD_STATE = 128
SSD_GN = SSD_GROUPS * SSD_STATE
SSD_CONV_DIM = SSD_INNER + 2 * SSD_GN
GDN_HEADS = 8
GDN_DK = 128
GDN_DV = 256
GDN_QK = GDN_HEADS * GDN_DK
GDN_VW = GDN_HEADS * GDN_DV
GDN_CONV_DIM = 2 * GDN_QK + GDN_VW
GDN_HPG = 4
GDN_GROUPS = GDN_HEADS // GDN_HPG
PEER_HEADS = 8
N_KEYS = 128
N_EXPERTS = N_KEYS * N_KEYS
PEER_TOPK = 16
PEER_DKEY = 256
PEER_HALF = PEER_DKEY // 2
PEER_SLOTS = PEER_HEADS * PEER_TOPK

LANES = 128
SUBLANES = 8
ROW_VREGS = D_MODEL // LANES
HALF_ROWS = ROW_VREGS // 2
VMEM_LIMIT = 56 * 1024 * 1024


def _cparams(sem):
    return pltpu.CompilerParams(dimension_semantics=sem, vmem_limit_bytes=VMEM_LIMIT)


def _const_spec(shape):
    nd = len(shape)
    return pl.BlockSpec(shape, lambda *_: (0,) * nd, pipeline_mode=pl.Buffered(1))


def _mod_spec(bm):
    if bm == 1:
        return pl.BlockSpec((1, 1, D_MODEL), lambda b, *_: (0, 0, 0))
    return pl.BlockSpec((1, 1, D_MODEL), lambda b, *_: (b, 0, 0))


def _silu(x):
    return x * jax.nn.sigmoid(x)


def _split_bf16(a):
    hi = a.astype(BF16)
    lo = (a - hi.astype(F32)).astype(BF16)
    return hi, lo


def _dot(a, b):
    return jnp.dot(a, b, preferred_element_type=F32)


def _dot_nt(a, b):
    return lax.dot_general(a, b, (((1,), (1,)), ((), ())), preferred_element_type=F32)


def _adaln(x, g, shift, scale):
    ms = jnp.mean(x * x, axis=-1, keepdims=True)
    return x * lax.rsqrt(ms + EPS) * g * (1.0 + scale) + shift


def _mod_body(c_ref, w_ref, b_ref, o_ref):
    s = _silu(c_ref[...])
    o_ref[0] = jnp.dot(s, w_ref[0], precision=HIGHEST, preferred_element_type=F32) + b_ref[0]


def _modulation(c_all, w_mod, b_mod):
    depth, d, n = w_mod.shape
    rows = c_all.shape[0]
    tn = 1536
    return pl.pallas_call(
        _mod_body,
        grid=(depth, n // tn),
        in_specs=[
            pl.BlockSpec((rows, d), lambda i, j: (0, 0)),
            pl.BlockSpec((1, d, tn), lambda i, j: (i, 0, j)),
            pl.BlockSpec((1, 1, tn), lambda i, j: (i, 0, j)),
        ],
        out_specs=pl.BlockSpec((1, rows, tn), lambda i, j: (i, 0, j)),
        out_shape=jax.ShapeDtypeStruct((depth, rows, n), F32),
        compiler_params=_cparams(("arbitrary", "arbitrary")),
        name="modulation",
    )(c_all, w_mod, b_mod.reshape(depth, 1, n))


def _grid_pos_embed(n):
    rows = n // GRID_W
    r = jnp.repeat(jnp.arange(rows, dtype=F32), GRID_W)
    col = jnp.tile(jnp.arange(GRID_W, dtype=F32), rows)
    quarter = D_MODEL // 4
    omega = 1.0 / (10000.0 ** (jnp.arange(quarter, dtype=F32) / quarter))

    def enc(p):
        ang = p[:, None] * omega[None, :]
        return jnp.concatenate([jnp.sin(ang), jnp.cos(ang)], axis=-1)

    return jnp.concatenate([enc(r), enc(col)], axis=-1)


def _add_body(x_ref, p_ref, o_ref):
    o_ref[0] = x_ref[0] + p_ref[...]


def _add_pos(x, pe):
    b, l, d = x.shape
    tl = 512
    return pl.pallas_call(
        _add_body,
        grid=(l // tl, b),
        in_specs=[pl.BlockSpec((1, tl, d), lambda i, j: (j, i, 0)),
                  pl.BlockSpec((tl, d), lambda i, j: (i, 0))],
        out_specs=pl.BlockSpec((1, tl, d), lambda i, j: (j, i, 0)),
        out_shape=jax.ShapeDtypeStruct(x.shape, F32),
        compiler_params=_cparams(("arbitrary", "arbitrary")),
        name="add_pos",
    )(x, pe)


def _norm_proj_body(x_ref, g_ref, sh_ref, sc_ref, *refs, n_w):
    h = _adaln(x_ref[0], g_ref[...], sh_ref[0], sc_ref[0]).astype(BF16)
    for w_ref, o_ref in zip(refs[:n_w], refs[n_w:]):
        o_ref[0] = _dot(h, w_ref[...]).astype(o_ref.dtype)


def _norm_proj(x, g, shift, scale, ws, dtypes):
    b, l, d = x.shape
    tl = min(l, 256)
    bm = shift.shape[0]
    in_specs = [
        pl.BlockSpec((1, tl, d), lambda i, j: (i, j, 0)),
        _const_spec((1, d)),
        _mod_spec(bm),
        _mod_spec(bm),
    ] + [_const_spec(w.shape) for w in ws]
    return pl.pallas_call(
        functools.partial(_norm_proj_body, n_w=len(ws)),
        grid=(b, l // tl),
        in_specs=in_specs,
        out_specs=[pl.BlockSpec((1, tl, w.shape[1]), lambda i, j: (i, j, 0)) for w in ws],
        out_shape=[jax.ShapeDtypeStruct((b, l, w.shape[1]), dt) for w, dt in zip(ws, dtypes)],
        compiler_params=_cparams(("arbitrary", "arbitrary")),
        name="norm_proj",
    )(x, g.reshape(1, d), shift, scale, *ws)


def _conv_body(x_ref, xm_ref, xp_ref, w_ref, b_ref, o_ref, *, n_l2, n_qscale, hd):
    i = pl.program_id(1)
    j = pl.program_id(2)
    x = x_ref[0].astype(F32)
    r, tc = x.shape
    row = lax.broadcasted_iota(I32, (r, tc), 0)
    halo = xm_ref.shape[1]
    prev_row = jnp.where(i == 0, 0.0, xm_ref[0].astype(F32)[halo - 1:halo, :])
    next_row = jnp.where(i == pl.num_programs(1) - 1, 0.0, xp_ref[0].astype(F32)[0:1, :])
    x_prev = jnp.where(row == 0, prev_row, pltpu.roll(x, 1, axis=0))
    x_next = jnp.where(row == r - 1, next_row, pltpu.roll(x, r - 1, axis=0))
    w = w_ref[...]
    y = _silu(x_prev * w[0:1] + x * w[1:2] + x_next * w[2:3] + b_ref[...])
    if n_l2 == 0:
        o_ref[0] = y.astype(o_ref.dtype)
        return

    @pl.when(j >= n_l2)
    def _():
        o_ref[0] = y.astype(o_ref.dtype)

    @pl.when(j < n_l2)
    def _():
        qs = jnp.where(j < n_qscale, hd ** -0.5, 1.0)
        for k in range(tc // hd):
            yk = y[:, k * hd:(k + 1) * hd]
            ss = jnp.sum(yk * yk, axis=-1, keepdims=True)
            o_ref[0, :, k * hd:(k + 1) * hd] = (yk * (lax.rsqrt(ss + EPS) * qs)).astype(o_ref.dtype)


def _conv_silu(x, w, bias, n_l2=0, n_qscale=0, hd=GDN_DK):
    b, l, c = x.shape
    r = min(l, 512)
    tc = 512
    halo = 2 * SUBLANES
    rs = r // halo
    last = l // halo - 1
    return pl.pallas_call(
        functools.partial(_conv_body, n_l2=n_l2, n_qscale=n_qscale, hd=hd),
        grid=(b, l // r, c // tc),
        in_specs=[
            pl.BlockSpec((1, r, tc), lambda bi, i, j: (bi, i, j)),
            pl.BlockSpec((1, halo, tc), lambda bi, i, j: (bi, jnp.maximum(i * rs - 1, 0), j)),
            pl.BlockSpec((1, halo, tc), lambda bi, i, j: (bi, jnp.minimum((i + 1) * rs, last), j)),
            pl.BlockSpec((3, tc), lambda bi, i, j: (0, j)),
            pl.BlockSpec((1, tc), lambda bi, i, j: (0, j)),
        ],
        out_specs=pl.BlockSpec((1, r, tc), lambda bi, i, j: (bi, i, j)),
        out_shape=jax.ShapeDtypeStruct(x.shape, x.dtype),
        compiler_params=_cparams(("arbitrary", "arbitrary", "arbitrary")),
        name="conv_silu",
    )(x, x, x, w, bias.reshape(1, c))


def _chunk_masks(fwd):
    ii = lax.broadcasted_iota(I32, (CHUNK, CHUNK), 0)
    jj = lax.broadcasted_iota(I32, (CHUNK, CHUNK), 1)
    incl = (ii >= jj) if fwd else (ii <= jj)
    strict = (ii > jj) if fwd else (ii < jj)
    return incl, strict


def _cumsum_chunk(incl, v):
    return jnp.dot(incl.astype(F32), v, precision=HIGHEST, preferred_element_type=F32)


def _softplus(x):
    return jnp.maximum(x, 0.0) + jnp.log1p(jnp.exp(-jnp.abs(x)))


def _ssd_chunk(dirs, bias, a, spread_ref, diag_ref):
    units = []
    for d, (x_ref, b_ref, c_ref, dt_ref, s_ref, y_ref, col0, fwd) in enumerate(dirs):
        incl, _ = _chunk_masks(fwd)
        bm_t = b_ref[0].astype(F32).T.astype(BF16)
        cm = c_ref[0]
        dt = _softplus(dt_ref[0] + bias)
        acs = _cumsum_chunk(incl, dt * a)
        acs_t = acs.T
        tot = acs[CHUNK - 1:CHUNK] if fwd else acs[0:1]
        cols = jnp.concatenate([dt, jnp.exp(tot - acs), jnp.exp(acs),
                                jnp.broadcast_to(jnp.exp(tot), (SUBLANES, LANES))], axis=0)
        units.append(dict(
            x_ref=x_ref, y_ref=y_ref, s_ref=s_ref, bm_t=bm_t, cm=cm, cols=cols, spread=spread_ref[d],
            cb=_dot(cm, bm_t),
            lms=[jnp.exp(jnp.where(incl, acs[:, col0 + k:col0 + k + 1] - acs_t[col0 + k:col0 + k + 1, :], -jnp.inf))
                 for k in range(SSD_HPG)]))
    wide = []
    for u in units:
        h1 = u["cols"].astype(BF16)
        r1 = u["cols"] - h1.astype(F32)
        h2 = r1.astype(BF16)
        h3 = (r1 - h2.astype(F32)).astype(BF16)
        wide.append(_dot(h1, u["spread"]) + _dot(h2, u["spread"]) + _dot(h3, u["spread"]))
    q = CHUNK
    xdt = [u["x_ref"][0] * w[0:q] for u, w in zip(units, wide)]
    m_cat = [jnp.concatenate([u["cb"] * lm for lm in u["lms"]], axis=1).astype(BF16) for u in units]
    y_in = [_dot(m, jnp.tile(xd.astype(BF16), (SSD_HPG, 1)) * diag_ref[...]) for m, xd in zip(m_cat, xdt)]
    ss = [u["s_ref"][...] for u in units]
    y_x = [_dot(u["cm"], s.astype(BF16)) * w[2 * q:3 * q] for u, s, w in zip(units, ss, wide)]
    for u, yi, yx in zip(units, y_in, y_x):
        u["y_ref"][0] = (yi + yx).astype(BF16)
    new = [s * w[3 * q:3 * q + 1] + _dot(u["bm_t"], (xd * w[q:2 * q]).astype(BF16))
           for u, s, w, xd in zip(units, ss, wide, xdt)]
    for u, s in zip(units, new):
        u["s_ref"][...] = s


def _load_states(states, s0_ref):
    for k, s in enumerate(states):
        s[...] = jnp.zeros_like(s) if s0_ref is None else s0_ref[0, k]


def _store_states(out_ref, states):
    for k, s in enumerate(states):
        out_ref[0, k] = s[...]


def _ssd_body(*refs, has_state):
    (xf, bf, cf, dtf, xb, bb, cbk, dtb, bias_ref, alog_ref, spread_ref, diag_ref), refs = refs[:12], refs[12:]
    s0f = s0b = None
    if has_state:
        (s0f, s0b), refs = refs[:2], refs[2:]
    yf, yb, sfo, sbo, sf, sb = refs
    c = pl.program_id(2)
    p = SSD_HEAD_DIM

    @pl.when(c == 0)
    def _():
        for s, s0 in ((sf, s0f), (sb, s0b)):
            for k in range(SSD_HPG):
                s[:, k * p:(k + 1) * p] = jnp.zeros((SSD_STATE, p), F32) if s0 is None else s0[0, k].T

    bias = bias_ref[0]
    a = -jnp.exp(alog_ref[0])
    _ssd_chunk([(xf, bf, cf, dtf, sf, yf, 0, True), (xb, bb, cbk, dtb, sb, yb, SSD_HPG, False)],
               bias, a, spread_ref, diag_ref)

    @pl.when(c == pl.num_programs(2) - 1)
    def _():
        for s, out in ((sf, sfo), (sb, sbo)):
            for k in range(SSD_HPG):
                out[0, k] = s[:, k * p:(k + 1) * p].T


def _ssd_scan(xbc, dt, bias, alog, s0f=None, s0b=None):
    b, l, _ = xbc.shape
    nc = l // CHUNK
    gw = SSD_HPG * SSD_HEAD_DIM
    boff = SSD_INNER // SSD_STATE
    coff = boff + SSD_GROUPS
    has_state = s0f is not None

    def seq_specs(ci):
        return [
            pl.BlockSpec((1, CHUNK, gw), lambda bi, g, c: (bi, ci(c), g)),
            pl.BlockSpec((1, CHUNK, SSD_STATE), lambda bi, g, c: (bi, ci(c), boff + g)),
            pl.BlockSpec((1, CHUNK, SSD_STATE), lambda bi, g, c: (bi, ci(c), coff + g)),
            pl.BlockSpec((1, CHUNK, LANES), lambda bi, g, c: (bi, ci(c), g)),
        ]

    fw = lambda c: c
    bw = lambda c: nc - 1 - c
    st_spec = pl.BlockSpec((1, SSD_HPG, SSD_HEAD_DIM, SSD_STATE), lambda bi, g, c: (bi, g, 0, 0))
    par_spec = pl.BlockSpec((1, 1, LANES), lambda bi, g, c: (g, 0, 0))
    head_of_lane = jnp.arange(gw) // SSD_HEAD_DIM
    lane_col = jnp.arange(LANES)[:, None]
    spread = jnp.stack([lane_col == head_of_lane[None, :], lane_col == head_of_lane[None, :] + SSD_HPG]).astype(BF16)
    diag = (jnp.arange(SSD_HPG * CHUNK)[:, None] // CHUNK == head_of_lane[None, :]).astype(BF16)
    in_specs = seq_specs(fw) + seq_specs(bw) + [par_spec, par_spec, _const_spec(spread.shape), _const_spec(diag.shape)]
    args = [xbc, xbc, xbc, dt, xbc, xbc, xbc, dt, bias, alog, spread, diag]
    if has_state:
        in_specs += [st_spec, st_spec]
        args += [s0f, s0b]
    st_shape = jax.ShapeDtypeStruct((b, SSD_HEADS, SSD_HEAD_DIM, SSD_STATE), F32)
    y_shape = jax.ShapeDtypeStruct((b, l, SSD_INNER), BF16)
    return pl.pallas_call(
        functools.partial(_ssd_body, has_state=has_state),
        grid=(b, SSD_GROUPS, nc),
        in_specs=in_specs,
        out_specs=[
            pl.BlockSpec((1, CHUNK, gw), lambda bi, g, c: (bi, c, g)),
            pl.BlockSpec((1, CHUNK, gw), lambda bi, g, c: (bi, nc - 1 - c, g)),
            st_spec, st_spec,
        ],
        out_shape=[y_shape, y_shape, st_shape, st_shape],
        scratch_shapes=[pltpu.VMEM((SSD_STATE, gw), F32)] * 2,
        compiler_params=_cparams(("arbitrary", "arbitrary", "arbitrary")),
        name="ssd_scan",
    )(*args)


def _ssd_out_body(yf_ref, yb_ref, xs_ref, z_ref, d_ref, ng_ref, w_ref, x_ref, gate_ref, o_ref):
    y = yf_ref[0].astype(F32) + yb_ref[0].astype(F32) + xs_ref[0] * d_ref[...]
    y = y * _silu(z_ref[0].astype(F32))
    y = y * lax.rsqrt(jnp.mean(y * y, axis=-1, keepdims=True) + EPS) * ng_ref[...]
    o_ref[0] = x_ref[0] + gate_ref[0] * _dot(y.astype(BF16), w_ref[...])


def _ssd_out(yf, yb, xbc, z, d_skip, norm_g, w_out, x, gate):
    b, l, d = x.shape
    tl = min(l, 256)
    inner = SSD_INNER
    tok = lambda n: pl.BlockSpec((1, tl, n), lambda i, j: (i, j, 0))
    return pl.pallas_call(
        _ssd_out_body,
        grid=(b, l // tl),
        in_specs=[tok(inner), tok(inner), tok(inner), tok(inner),
                  _const_spec((1, inner)), _const_spec((1, inner)), _const_spec(w_out.shape),
                  tok(d), _mod_spec(gate.shape[0])],
        out_specs=tok(d),
        out_shape=jax.ShapeDtypeStruct(x.shape, F32),
        compiler_params=_cparams(("arbitrary", "arbitrary")),
        name="ssd_out",
    )(yf, yb, xbc, z, d_skip, norm_g, w_out, x, gate)


def _mm(a, b):
    return _dot(a.astype(BF16), b.astype(BF16))


def _mm_nt(a, b):
    return _dot_nt(a.astype(BF16), b.astype(BF16))


INV_BASE_LOG2 = 3


def _unit_tri_inverses(ms):
    ii = lax.broadcasted_iota(I32, (CHUNK, CHUNK), 0)
    jj = lax.broadcasted_iota(I32, (CHUNK, CHUNK), 1)
    eye = jnp.where(ii == jj, 1.0, 0.0)
    base = (ii >> INV_BASE_LOG2) == (jj >> INV_BASE_LOG2)
    pws = [jnp.where(base, m, 0.0) for m in ms]
    xs = [eye - d for d in pws]
    for _ in range(INV_BASE_LOG2 - 1):
        pws = [_mm(pw, pw) for pw in pws]
        xs = [x + _mm(x, pw) for x, pw in zip(xs, pws)]
    for lg in range(INV_BASE_LOG2, int(math.log2(CHUNK))):
        off_diag = ((ii >> (lg + 1)) == (jj >> (lg + 1))) & ((ii >> lg) != (jj >> lg))
        cxs = [_mm(jnp.where(off_diag, m, 0.0), x) for m, x in zip(ms, xs)]
        xs = [x - _mm(x, cx) for x, cx in zip(xs, cxs)]
    return xs


def _gdn_chunk(dirs, bias, neg_a):
    dk, dv = GDN_DK, GDN_DV
    units = []
    for q_ref, k_ref, v_ref, ab_ref, states, o_ref, off, fwd in dirs:
        incl, strict = _chunk_masks(fwd)
        q = q_ref[0]
        k = k_ref[0]
        v = v_ref[0]
        ab = ab_ref[0]
        beta_all = jax.nn.sigmoid(ab)
        gc = _cumsum_chunk(incl, neg_a * _softplus(ab + bias))
        gc_t = gc.T
        tot = gc[CHUNK - 1:CHUNK] if fwd else gc[0:1]
        e_gc = jnp.exp(gc)
        e_rest = jnp.exp(tot - gc)
        g_last = jnp.exp(tot)
        for h in range(GDN_HPG):
            cb = off + h
            ca = 2 * GDN_HPG + off + h
            kh = k[:, h * dk:(h + 1) * dk]
            beta = beta_all[:, cb:cb + 1]
            units.append(dict(
                incl=incl, strict=strict, kh=kh, kb=kh * beta,
                qh=q[:, h * dk:(h + 1) * dk], vb=v[:, h * dv:(h + 1) * dv] * beta,
                decay=jnp.exp(jnp.where(incl, gc[:, ca:ca + 1] - gc_t[ca:ca + 1, :], -jnp.inf)),
                e_gc=e_gc[:, ca:ca + 1], e_rest=e_rest[:, ca:ca + 1], g_last=g_last[:, ca:ca + 1],
                s_ref=states[h], o_ref=o_ref, cols=slice(h * dv, (h + 1) * dv)))
    ms = [jnp.where(u["strict"], _mm_nt(u["kb"], u["kh"]) * u["decay"], 0.0) for u in units]
    ts = _unit_tri_inverses(ms)
    us = [_mm(t, u["vb"]) for t, u in zip(ts, units)]
    ws = [_mm(t, u["kb"] * u["e_gc"]) for t, u in zip(ts, units)]
    aqk = [jnp.where(u["incl"], _mm_nt(u["qh"], u["kh"]) * u["decay"], 0.0) for u in units]
    ss = [u["s_ref"][...] for u in units]
    vn = [x - _mm(w, s) for x, w, s in zip(us, ws, ss)]
    os_ = [_mm(u["qh"] * u["e_gc"], s) + _mm(a, v) for u, s, a, v in zip(units, ss, aqk, vn)]
    for u, o in zip(units, os_):
        u["o_ref"][0, :, u["cols"]] = o.astype(BF16)
    new = [s * u["g_last"] + _mm((u["kh"] * u["e_rest"]).T, v) for u, s, v in zip(units, ss, vn)]
    for u, s in zip(units, new):
        u["s_ref"][...] = s


def _gdn_body(*refs, has_state):
    (qf, kf, vf, abf, qb, kb, vb, abb, bias_ref, alog_ref), refs = refs[:10], refs[10:]
    s0f = s0b = None
    if has_state:
        (s0f, s0b), refs = refs[:2], refs[2:]
    (of, ob, sfo, sbo), states = refs[:4], refs[4:]
    sf, sb = states[:GDN_HPG], states[GDN_HPG:]
    c = pl.program_id(2)

    @pl.when(c == 0)
    def _():
        _load_states(sf, s0f)
        _load_states(sb, s0b)

    bias = bias_ref[0]
    neg_a = -jnp.exp(alog_ref[0])
    _gdn_chunk([(qf, kf, vf, abf, sf, of, 0, True), (qb, kb, vb, abb, sb, ob, GDN_HPG, False)], bias, neg_a)

    @pl.when(c == pl.num_programs(2) - 1)
    def _():
        _store_states(sfo, sf)
        _store_states(sbo, sb)


def _gdn_scan(qkv, ab, bias, alog, s0f=None, s0b=None):
    b, l, _ = qkv.shape
    nc = l // CHUNK
    qw = GDN_HPG * GDN_DK
    vw = GDN_HPG * GDN_DV
    koff = GDN_QK // qw
    voff = 2 * GDN_QK // vw
    has_state = s0f is not None

    def seq_specs(ci):
        return [
            pl.BlockSpec((1, CHUNK, qw), lambda bi, g, c: (bi, ci(c), g)),
            pl.BlockSpec((1, CHUNK, qw), lambda bi, g, c: (bi, ci(c), koff + g)),
            pl.BlockSpec((1, CHUNK, vw), lambda bi, g, c: (bi, ci(c), voff + g)),
            pl.BlockSpec((1, CHUNK, LANES), lambda bi, g, c: (bi, ci(c), g)),
        ]

    fw = lambda c: c
    bw = lambda c: nc - 1 - c
    st_spec = pl.BlockSpec((1, GDN_HPG, GDN_DK, GDN_DV), lambda bi, g, c: (bi, g, 0, 0))
    par_spec = pl.BlockSpec((1, 1, LANES), lambda bi, g, c: (g, 0, 0))
    in_specs = seq_specs(fw) + seq_specs(bw) + [par_spec, par_spec]
    args = [qkv, qkv, qkv, ab, qkv, qkv, qkv, ab, bias, alog]
    if has_state:
        in_specs += [st_spec, st_spec]
        args += [s0f, s0b]
    st_shape = jax.ShapeDtypeStruct((b, GDN_HEADS, GDN_DK, GDN_DV), F32)
    o_shape = jax.ShapeDtypeStruct((b, l, GDN_VW), BF16)
    return pl.pallas_call(
        functools.partial(_gdn_body, has_state=has_state),
        grid=(b, GDN_GROUPS, nc),
        in_specs=in_specs,
        out_specs=[
            pl.BlockSpec((1, CHUNK, vw), lambda bi, g, c: (bi, c, g)),
            pl.BlockSpec((1, CHUNK, vw), lambda bi, g, c: (bi, nc - 1 - c, g)),
            st_spec, st_spec,
        ],
        out_shape=[o_shape, o_shape, st_shape, st_shape],
        scratch_shapes=[pltpu.VMEM((GDN_DK, GDN_DV), F32)] * (2 * GDN_HPG),
        compiler_params=_cparams(("arbitrary", "arbitrary", "arbitrary")),
        name="gdn_scan",
    )(*args)


def _gdn_out_body(of_ref, ob_ref, z_ref, ng_ref, w_ref, x_ref, gate_ref, o_ref, y_ref):
    dv = GDN_DV
    for h in range(GDN_HEADS):
        sl = slice(h * dv, (h + 1) * dv)
        o = of_ref[0, :, sl].astype(F32) + ob_ref[0, :, sl].astype(F32)
        o = o * lax.rsqrt(jnp.mean(o * o, axis=-1, keepdims=True) + EPS) * ng_ref[...]
        y_ref[:, sl] = (o * _silu(z_ref[0, :, sl].astype(F32))).astype(BF16)
    o_ref[0] = x_ref[0] + gate_ref[0] * _dot(y_ref[...], w_ref[...])


def _gdn_out(of, ob, z, norm_g, w_out, x, gate):
    b, l, d = x.shape
    tl = min(l, 256)
    tok = lambda n: pl.BlockSpec((1, tl, n), lambda i, j: (i, j, 0))
    return pl.pallas_call(
        _gdn_out_body,
        grid=(b, l // tl),
        in_specs=[tok(GDN_VW), tok(GDN_VW), tok(GDN_VW),
                  _const_spec((1, GDN_DV)), _const_spec(w_out.shape),
                  tok(d), _mod_spec(gate.shape[0])],
        out_specs=tok(d),
        out_shape=jax.ShapeDtypeStruct(x.shape, F32),
        scratch_shapes=[pltpu.VMEM((tl, GDN_VW), BF16)],
        compiler_params=_cparams(("arbitrary", "arbitrary")),
        name="gdn_out",
    )(of, ob, z, norm_g, w_out, x, gate)


def _topk_chunks(sc, idc, k, fill):
    vals, idxs = [], []
    for _ in range(k):
        lv = list(zip(sc, idc))
        while len(lv) > 1:
            nxt = []
            for a in range(0, len(lv) - 1, 2):
                (va, ia), (vb, ib) = lv[a], lv[a + 1]
                nxt.append((jnp.maximum(va, vb), jnp.where(vb > va, ib, ia)))
            if len(lv) % 2:
                nxt.append(lv[-1])
            lv = nxt
        v8, i8 = lv[0]
        m = jnp.max(v8, axis=0, keepdims=True)
        idx = jnp.min(jnp.where(v8 == m, i8, fill), axis=0, keepdims=True)
        vals.append(m)
        idxs.append(idx)
        sc = [jnp.where(i == idx, -jnp.inf, v) for v, i in zip(sc, idc)]
    return jnp.concatenate(vals, axis=0), jnp.concatenate(idxs, axis=0).astype(I32)


def _topk_rows(s, k):
    n, t = s.shape
    row = lax.broadcasted_iota(I32, (SUBLANES, t), 0).astype(F32)
    sc = [s[r:r + SUBLANES] for r in range(0, n, SUBLANES)]
    return _topk_chunks(sc, [row + float(r) for r in range(0, n, SUBLANES)], k, float(n - 1))


def _topk_pairs(v0, v1):
    k = PEER_TOPK
    t = v0.shape[1]
    row = lax.broadcasted_iota(I32, (SUBLANES, t), 0).astype(F32)
    sc = [v0[0:1] + v1[0:SUBLANES], v0[0:1] + v1[SUBLANES:]]
    idc = [row, row + float(SUBLANES)]
    for a in range(1, SUBLANES):
        s = v0[a:a + 1] + v1[0:SUBLANES]
        sc.append(jnp.where(row < float(k // (a + 1)), s, -jnp.inf))
        idc.append(row + float(a * k))
    sc.append(v0[SUBLANES:] + v1[0:1])
    idc.append(row * float(k) + float(SUBLANES * k))
    return _topk_chunks(sc, idc, k, float(k * k))


def _pick_rows(sel, table):
    out = jnp.zeros(sel.shape, table.dtype)
    for a in range(table.shape[0]):
        out = jnp.where(sel == a, table[a:a + 1], out)
    return out


def _peer_select_body(x_ref, g_ref, sh_ref, sc_ref, wqh_ref, wql_ref, kh_ref, kl_ref,
                      h_ref, eid_ref, gate_ref):
    h = _adaln(x_ref[0], g_ref[...], sh_ref[0], sc_ref[0])
    for m in range(ROW_VREGS):
        h_ref[0, pl.ds(m, h.shape[0], stride=ROW_VREGS), :] = h[:, m * LANES:(m + 1) * LANES]
    hh, hl = _split_bf16(h)
    q = _dot(hh, wqh_ref[...]) + _dot(hh, wql_ref[...]) + _dot(hl, wqh_ref[...])
    eids, gates = [], []
    for hd in range(PEER_HEADS):
        sv, si = [], []
        for s in range(2):
            c0 = hd * PEER_DKEY + s * PEER_HALF
            qh, ql = _split_bf16(q[:, c0:c0 + PEER_HALF])
            kh = kh_ref[s, hd]
            st = _dot_nt(kh, qh) + _dot_nt(kh, ql) + _dot_nt(kl_ref[s, hd], qh)
            v, i = _topk_rows(st, PEER_TOPK)
            sv.append(v)
            si.append(i)
        cv, ci = _topk_pairs(sv[0], sv[1])
        shift = int(math.log2(PEER_TOPK))
        i1 = _pick_rows(ci >> shift, si[0])
        i2 = _pick_rows(ci & (PEER_TOPK - 1), si[1])
        eids.append(i1 * N_KEYS + i2)
        e = jnp.exp(cv - cv[0:1])
        gates.append(e / jnp.sum(e, axis=0, keepdims=True))
    eid_ref[0] = jnp.concatenate(eids, axis=0).astype(F32).T.astype(I32)
    gate_ref[0] = jnp.concatenate(gates, axis=0).T


def _peer_select(x, g, shift, scale, wq_hi, wq_lo, k_hi, k_lo):
    b, l, d = x.shape
    tl = min(l, 256)
    bm = shift.shape[0]
    tok = lambda n: pl.BlockSpec((1, tl, n), lambda i, j: (i, j, 0))
    return pl.pallas_call(
        _peer_select_body,
        grid=(b, l // tl),
        in_specs=[tok(d), _const_spec((1, d)), _mod_spec(bm), _mod_spec(bm),
                  _const_spec(wq_hi.shape), _const_spec(wq_lo.shape),
                  _const_spec(k_hi.shape), _const_spec(k_lo.shape)],
        out_specs=[pl.BlockSpec((1, tl * ROW_VREGS, LANES), lambda i, j: (i, j, 0)),
                   tok(PEER_SLOTS), tok(PEER_SLOTS)],
        out_shape=[jax.ShapeDtypeStruct((b, l * ROW_VREGS, LANES), F32),
                   jax.ShapeDtypeStruct((b, l, PEER_SLOTS), I32),
                   jax.ShapeDtypeStruct((b, l, PEER_SLOTS), F32)],
        compiler_params=_cparams(("arbitrary", "arbitrary")),
        name="peer_select",
    )(x, g.reshape(1, d), shift, scale, wq_hi, wq_lo, k_hi, k_lo)


def _pack_body(t_ref, o_ref):
    t = t_ref[...]
    half = t.shape[1] // 2
    lo = lax.bitcast_convert_type(t[:, :half].astype(BF16).astype(F32), jnp.uint32)
    hi = lax.bitcast_convert_type(t[:, half:].astype(BF16).astype(F32), jnp.uint32)
    word = (hi & jnp.uint32(0xFFFF0000)) | (lo >> 16)
    o_ref[...] = lax.bitcast_convert_type(word, I32)


def _pack_table(tab):
    e, d = tab.shape
    r = 512
    out = pl.pallas_call(
        _pack_body,
        grid=(e // r,),
        in_specs=[pl.BlockSpec((r, d), lambda i: (i, 0))],
        out_specs=pl.BlockSpec((r, d // 2), lambda i: (i, 0)),
        out_shape=jax.ShapeDtypeStruct((e, d // 2), I32),
        compiler_params=_cparams(("arbitrary",)),
        name="pack_table",
    )(tab)
    return out.reshape(e, HALF_ROWS, LANES)


def _unpack_words(w):
    lo = lax.bitcast_convert_type(w << 16, F32)
    hi = lax.bitcast_convert_type(w & jnp.int32(-65536), F32)
    return lo, hi


PEER_LAG = 2


def _peer_u_body(eid_ref, hr_ref, gate_ref, tab_ref, act_ref, ps_a, ps_b, acc_ref, *, tb):
    ones = jnp.ones((LANES, LANES), BF16)
    lane = lax.broadcasted_iota(I32, (PEER_SLOTS, tb), 1)

    def gather(t, ps_ref):
        xr = hr_ref[0, pl.ds(pl.multiple_of(t * ROW_VREGS, ROW_VREGS), ROW_VREGS), :]
        x_lo = xr[0:HALF_ROWS]
        x_hi = xr[HALF_ROWS:ROW_VREGS]
        for j in range(PEER_SLOTS):
            lo, hi = _unpack_words(tab_ref[eid_ref[0, t, j]])
            ps_ref[j * HALF_ROWS:(j + 1) * HALF_ROWS, :] = lo * x_lo + hi * x_hi

    def reduce(ps_ref, tok):
        part = ps_ref[pl.ds(0, PEER_SLOTS, stride=HALF_ROWS), :]
        for s in range(1, HALF_ROWS):
            part = part + ps_ref[pl.ds(s, PEER_SLOTS, stride=HALF_ROWS), :]
        p_hi, p_lo = _split_bf16(part)
        sums = _dot(p_hi, ones) + _dot(p_lo, ones)
        acc_ref[...] = jnp.where(lane == tok, sums, acc_ref[...])

    ps_a[...] = jnp.zeros_like(ps_a)
    ps_b[...] = jnp.zeros_like(ps_b)
    acc_ref[...] = jnp.zeros_like(acc_ref)

    def pair(i, carry):
        t = i * PEER_LAG
        reduce(ps_a, t - PEER_LAG)
        reduce(ps_b, t - PEER_LAG + 1)
        gather(t, ps_a)
        gather(t + 1, ps_b)
        return carry

    lax.fori_loop(0, tb // PEER_LAG, pair, 0)
    reduce(ps_a, tb - PEER_LAG)
    reduce(ps_b, tb - PEER_LAG + 1)
    a = acc_ref[...].T
    act_ref[0] = 0.5 * a * (1.0 + lax.erf(a * (2.0 ** -0.5))) * gate_ref[0]


def _peer_v_body(eid_ref, act_ref, x_ref, gate_ref, eye_ref, tab_ref, o_ref, wb_a, wb_b, res_ref, *, tb):
    ones = jnp.ones((PEER_SLOTS, LANES), BF16)

    def spread(t, wb_ref):
        a = act_ref[0, pl.ds(t, 1), :]
        diag = jnp.where(eye_ref[...] != 0.0, a, 0.0)
        wb_ref[...] = _dot(diag.astype(BF16), ones)

    def combine(t, wb_ref):
        n_acc = 2
        acc_lo = [jnp.zeros((HALF_ROWS, LANES), F32) for _ in range(n_acc)]
        acc_hi = [jnp.zeros((HALF_ROWS, LANES), F32) for _ in range(n_acc)]
        for j in range(PEER_SLOTS):
            lo, hi = _unpack_words(tab_ref[eid_ref[0, t, j]])
            wj = wb_ref[j:j + 1, :]
            acc_lo[j % n_acc] = acc_lo[j % n_acc] + lo * wj
            acc_hi[j % n_acc] = acc_hi[j % n_acc] + hi * wj
        rows = pl.ds(pl.multiple_of(t * ROW_VREGS, ROW_VREGS), ROW_VREGS)
        res_ref[rows, :] = jnp.concatenate([acc_lo[0] + acc_lo[1], acc_hi[0] + acc_hi[1]], axis=0)

    spread(0, wb_a)

    def pair(i, carry):
        t = i * PEER_LAG
        spread(t + 1, wb_b)
        combine(t, wb_a)
        spread(jnp.minimum(t + PEER_LAG, tb - 1), wb_a)
        combine(t + 1, wb_b)
        return carry

    lax.fori_loop(0, tb // PEER_LAG, pair, 0)
    for m in range(ROW_VREGS):
        cols = slice(m * LANES, (m + 1) * LANES)
        o_ref[0, :, cols] = x_ref[0, :, cols] + gate_ref[0, :, cols] * res_ref[pl.ds(m, tb, stride=ROW_VREGS), :]


def _peer_retrieve(x, hr, eid, gate_w, mod_gate, tab_u, tab_v):
    b, l, d = x.shape
    tb = 128
    tok = pl.BlockSpec((1, tb, PEER_SLOTS), lambda i, j: (i, j, 0))
    eid_spec = pl.BlockSpec((1, tb, PEER_SLOTS), lambda i, j: (i, j, 0), memory_space=pltpu.SMEM)
    row_spec = pl.BlockSpec((1, tb * ROW_VREGS, LANES), lambda i, j: (i, j, 0))
    x_spec = pl.BlockSpec((1, tb, d), lambda i, j: (i, j, 0))
    tab_spec = pl.BlockSpec(memory_space=pltpu.VMEM)
    assert tb == LANES
    eye = jnp.eye(PEER_SLOTS, dtype=F32)
    part = pltpu.VMEM((PEER_SLOTS * HALF_ROWS, LANES), F32)
    act = pl.pallas_call(
        functools.partial(_peer_u_body, tb=tb),
        grid=(b, l // tb),
        in_specs=[eid_spec, row_spec, tok, tab_spec],
        out_specs=tok,
        out_shape=jax.ShapeDtypeStruct((b, l, PEER_SLOTS), F32),
        scratch_shapes=[part, part, pltpu.VMEM((PEER_SLOTS, tb), F32)],
        compiler_params=_cparams(("arbitrary", "arbitrary")),
        name="peer_u",
    )(eid, hr, gate_w, tab_u)
    spread = pltpu.VMEM((PEER_SLOTS, LANES), F32)
    return pl.pallas_call(
        functools.partial(_peer_v_body, tb=tb),
        grid=(b, l // tb),
        in_specs=[eid_spec, tok, x_spec, _mod_spec(mod_gate.shape[0]), _const_spec(eye.shape), tab_spec],
        out_specs=x_spec,
        out_shape=jax.ShapeDtypeStruct(x.shape, F32),
        scratch_shapes=[spread, spread, pltpu.VMEM((tb * ROW_VREGS, LANES), F32)],
        compiler_params=_cparams(("arbitrary", "arbitrary")),
        name="peer_v",
    )(eid, act, x, mod_gate, eye, tab_v)


def _final_norm_body(x_ref, g_ref, o_ref):
    x = x_ref[0]
    o_ref[0] = x * lax.rsqrt(jnp.mean(x * x, axis=-1, keepdims=True) + EPS) * g_ref[...]


def _final_norm(x, g):
    b, l, d = x.shape
    tl = min(l, 512)
    return pl.pallas_call(
        _final_norm_body,
        grid=(b, l // tl),
        in_specs=[pl.BlockSpec((1, tl, d), lambda i, j: (i, j, 0)), _const_spec((1, d))],
        out_specs=pl.BlockSpec((1, tl, d), lambda i, j: (i, j, 0)),
        out_shape=jax.ShapeDtypeStruct(x.shape, F32),
        compiler_params=_cparams(("arbitrary", "arbitrary")),
        name="final_norm",
    )(x, g.reshape(1, d))


def _group_blocked(cols, per_group, groups):
    d = cols[0].shape[0]
    blocks = []
    for g in range(groups):
        parts = [c[:, g * per_group:(g + 1) * per_group] for c in cols]
        used = per_group * len(cols)
        blocks.append(jnp.concatenate(parts + [jnp.zeros((d, LANES - used), cols[0].dtype)], axis=1))
    return jnp.concatenate(blocks, axis=1)


def _group_blocked_vec(vecs, per_group, groups):
    rows = []
    for g in range(groups):
        parts = [jnp.zeros((per_group,), F32) if v is None else v[g * per_group:(g + 1) * per_group]
                 for v in vecs]
        used = per_group * len(vecs)
        rows.append(jnp.concatenate(parts + [jnp.zeros((LANES - used,), F32)]))
    return jnp.stack(rows).reshape(groups, 1, LANES)


def _ssd_params(w_in, conv_w, conv_b, dt_bias, a_log, d_skip, norm_g, w_out):
    dt0 = SSD_INNER + SSD_CONV_DIM
    w_dt = _group_blocked([w_in[:, dt0:dt0 + SSD_HEADS], w_in[:, dt0 + SSD_HEADS:]], SSD_HPG, SSD_GROUPS)
    return dict(
        ws=[w_in[:, :SSD_INNER].astype(BF16), w_in[:, SSD_INNER:dt0].astype(BF16), w_dt.astype(BF16)],
        conv_w=conv_w, conv_b=conv_b,
        bias=_group_blocked_vec([dt_bias[0], dt_bias[1]], SSD_HPG, SSD_GROUPS),
        alog=_group_blocked_vec([a_log[0], a_log[1]], SSD_HPG, SSD_GROUPS),
        d_skip=jnp.repeat(d_skip, SSD_HEAD_DIM).reshape(1, SSD_INNER),
        norm_g=norm_g.reshape(1, SSD_INNER),
        w_out=w_out.astype(BF16),
    )


def _gdn_params(w_in, conv_w, dt_bias, a_log, norm_g, w_out):
    ab0 = GDN_CONV_DIM + GDN_VW
    h = GDN_HEADS
    ab = [w_in[:, ab0 + i * h:ab0 + (i + 1) * h] for i in range(4)]
    return dict(
        ws=[w_in[:, :GDN_CONV_DIM].astype(BF16), w_in[:, GDN_CONV_DIM:ab0].astype(BF16),
            _group_blocked(ab, GDN_HPG, GDN_GROUPS).astype(BF16)],
        conv_w=conv_w,
        bias=_group_blocked_vec([None, None, dt_bias[0], dt_bias[1]], GDN_HPG, GDN_GROUPS),
        alog=_group_blocked_vec([None, None, a_log[0], a_log[1]], GDN_HPG, GDN_GROUPS),
        norm_g=norm_g.reshape(1, GDN_DV),
        w_out=w_out.astype(BF16),
    )


def _peer_params(w_q, keys, u_tab, v_tab):
    wq_hi, wq_lo = _split_bf16(w_q)
    k_hi, k_lo = _split_bf16(keys)
    return dict(wq_hi=wq_hi, wq_lo=wq_lo, k_hi=k_hi, k_lo=k_lo,
                tab_u=_pack_table(u_tab), tab_v=_pack_table(v_tab))


def _ssd_layer(x, m, g, p, s0f=None, s0b=None):
    z, xbc, dt = _norm_proj(x, g, m[0], m[1], p["ws"], [BF16, BF16, F32])
    xbc = _conv_silu(xbc, p["conv_w"], p["conv_b"])
    yf, yb, sf, sb = _ssd_scan(xbc, dt, p["bias"], p["alog"], s0f, s0b)
    x = _ssd_out(yf, yb, xbc, z, p["d_skip"], p["norm_g"], p["w_out"], x, m[2])
    return x, sf, sb


def _gdn_layer(x, m, g, p, s0f=None, s0b=None):
    qkv, z, ab = _norm_proj(x, g, m[0], m[1], p["ws"], [BF16, BF16, F32])
    qkv = _conv_silu(qkv, p["conv_w"], jnp.zeros((GDN_CONV_DIM,), F32),
                     n_l2=2 * GDN_QK // 512, n_qscale=GDN_QK // 512)
    of, ob, sf, sb = _gdn_scan(qkv, ab, p["bias"], p["alog"], s0f, s0b)
    x = _gdn_out(of, ob, z, p["norm_g"], p["w_out"], x, m[2])
    return x, sf, sb


def _peer_layer(x, m, g, p):
    h, eid, gate_w = _peer_select(x, g, m[3], m[4], p["wq_hi"], p["wq_lo"], p["k_hi"], p["k_lo"])
    return _peer_retrieve(x, h, eid, gate_w, m[5], p["tab_u"], p["tab_v"])


def kernel(x_prompt, x_sample, c, state_ssd_fwd, state_ssd_bwd, state_gdn_fwd, state_gdn_bwd, c_ctx, w_mod, b_mod, norm_mix_g, norm_ffn_g, ssd_w_in, ssd_conv_w, ssd_conv_b, ssd_dt_bias, ssd_a_log, ssd_d, ssd_norm_g, ssd_w_out, gdn_w_in, gdn_conv_w, gdn_dt_bias, gdn_a_log, gdn_norm_g, gdn_w_out, peer_w_q, peer_keys, peer_u, peer_v, final_norm_g):
    d = D_MODEL
    nb = c.shape[0]
    rows = 2 * SUBLANES
    c_all = jnp.zeros((rows, d), F32).at[0].set(c_ctx).at[1:1 + nb].set(c)
    mod = _modulation(c_all, w_mod, b_mod)
    xp = x_prompt
    xs = _add_pos(x_sample, _grid_pos_embed(x_sample.shape[1]))
    ssd_f, ssd_b, gdn_f, gdn_b = [], [], [], []
    for i in range(DEPTH):
        mp = [mod[i, 0:1, k * d:(k + 1) * d].reshape(1, 1, d) for k in range(6)]
        ms = [mod[i, 1:1 + nb, k * d:(k + 1) * d].reshape(nb, 1, d) for k in range(6)]
        j = i // 2
        if i % 2 == 0:
            p = _ssd_params(ssd_w_in[j], ssd_conv_w[j], ssd_conv_b[j], ssd_dt_bias[j], ssd_a_log[j],
                            ssd_d[j], ssd_norm_g[j], ssd_w_out[j])
            xp, sf, sb = _ssd_layer(xp, mp, norm_mix_g[i], p)
            xs, _, _ = _ssd_layer(xs, ms, norm_mix_g[i], p, state_ssd_fwd[:, j], state_ssd_bwd[:, j])
            ssd_f.append(sf)
            ssd_b.append(sb)
        else:
            p = _gdn_params(gdn_w_in[j], gdn_conv_w[j], gdn_dt_bias[j], gdn_a_log[j], gdn_norm_g[j], gdn_w_out[j])
            xp, sf, sb = _gdn_layer(xp, mp, norm_mix_g[i], p)
            xs, _, _ = _gdn_layer(xs, ms, norm_mix_g[i], p, state_gdn_fwd[:, j], state_gdn_bwd[:, j])
            gdn_f.append(sf)
            gdn_b.append(sb)
        pp = _peer_params(peer_w_q[i], peer_keys[i], peer_u[i], peer_v[i])
        xp = _peer_layer(xp, mp, norm_ffn_g[i], pp)
        xs = _peer_layer(xs, ms, norm_ffn_g[i], pp)
    y_prompt = _final_norm(xp, final_norm_g)
    y_sample = _final_norm(xs, final_norm_g)
    return (y_prompt, y_sample, jnp.stack(ssd_f, axis=1), jnp.stack(ssd_b, axis=1),
            jnp.stack(gdn_f, axis=1), jnp.stack(gdn_b, axis=1))
```

```python
import functools
import math

import jax
import jax.numpy as jnp
from jax import lax
from jax.experimental import pallas as pl
from jax.experimental.pallas import tpu as pltpu

F32 = jnp.float32
BF16 = jnp.bfloat16
I32 = jnp.int32
HIGHEST = lax.Precision.HIGHEST

D_MODEL = 1024
DEPTH = 2
GRID_W = 64
CHUNK = 64
EPS = 1e-6
SSD_INNER = 2 * D_MODEL
SSD_HEAD_DIM = 64
SSD_HEADS = SSD_INNER // SSD_HEAD_DIM
SSD_GROUPS = 4
SSD_HPG = SSD_HEADS // SSD_GROUPS
SSD_STATE = 128
SSD_GN = SSD_GROUPS * SSD_STATE
SSD_CONV_DIM = SSD_INNER + 2 * SSD_GN
GDN_HEADS = 8
GDN_DK = 128
GDN_DV = 256
GDN_QK = GDN_HEADS * GDN_DK
GDN_VW = GDN_HEADS * GDN_DV
GDN_CONV_DIM = 2 * GDN_QK + GDN_VW
GDN_HPG = 4
GDN_STEP_CHUNKS = 4
SSD_STEP_CHUNKS = 4
GDN_GROUPS = GDN_HEADS // GDN_HPG
PEER_HEADS = 8
N_KEYS = 128
N_EXPERTS = N_KEYS * N_KEYS
PEER_TOPK = 16
PEER_DKEY = 256
PEER_HALF = PEER_DKEY // 2
PEER_SLOTS = PEER_HEADS * PEER_TOPK

LANES = 128
SUBLANES = 8
ROW_VREGS = D_MODEL // LANES
HALF_ROWS = ROW_VREGS // 2
VMEM_LIMIT = 56 * 1024 * 1024


def _cparams(sem):
    return pltpu.CompilerParams(dimension_semantics=sem, vmem_limit_bytes=VMEM_LIMIT)


def _const_spec(shape):
    nd = len(shape)
    return pl.BlockSpec(shape, lambda *_: (0,) * nd, pipeline_mode=pl.Buffered(1))


def _mod_spec(bm):
    if bm == 1:
        return pl.BlockSpec((1, 1, D_MODEL), lambda b, *_: (0, 0, 0))
    return pl.BlockSpec((1, 1, D_MODEL), lambda b, *_: (b, 0, 0))


def _silu(x):
    return x * jax.nn.sigmoid(x)


def _split_bf16(a):
    hi = a.astype(BF16)
    lo = (a - hi.astype(F32)).astype(BF16)
    return hi, lo


def _dot(a, b):
    return jnp.dot(a, b, preferred_element_type=F32)


def _dot_nt(a, b):
    return lax.dot_general(a, b, (((1,), (1,)), ((), ())), preferred_element_type=F32)


def _adaln(x, g, shift, scale):
    ms = jnp.mean(x * x, axis=-1, keepdims=True)
    return x * lax.rsqrt(ms + EPS) * g * (1.0 + scale) + shift


def _mod_body(c_ref, w_ref, b_ref, o_ref):
    s = _silu(c_ref[...])
    o_ref[0] = jnp.dot(s, w_ref[0], precision=HIGHEST, preferred_element_type=F32) + b_ref[0]


def _modulation(c_all, w_mod, b_mod):
    depth, d, n = w_mod.shape
    rows = c_all.shape[0]
    tn = 1536
    return pl.pallas_call(
        _mod_body,
        grid=(depth, n // tn),
        in_specs=[
            pl.BlockSpec((rows, d), lambda i, j: (0, 0)),
            pl.BlockSpec((1, d, tn), lambda i, j: (i, 0, j)),
            pl.BlockSpec((1, 1, tn), lambda i, j: (i, 0, j)),
        ],
        out_specs=pl.BlockSpec((1, rows, tn), lambda i, j: (i, 0, j)),
        out_shape=jax.ShapeDtypeStruct((depth, rows, n), F32),
        compiler_params=_cparams(("arbitrary", "arbitrary")),
        name="modulation",
    )(c_all, w_mod, b_mod.reshape(depth, 1, n))


def _grid_pos_embed(n):
    rows = n // GRID_W
    r = jnp.repeat(jnp.arange(rows, dtype=F32), GRID_W)
    col = jnp.tile(jnp.arange(GRID_W, dtype=F32), rows)
    quarter = D_MODEL // 4
    omega = 1.0 / (10000.0 ** (jnp.arange(quarter, dtype=F32) / quarter))

    def enc(p):
        ang = p[:, None] * omega[None, :]
        return jnp.concatenate([jnp.sin(ang), jnp.cos(ang)], axis=-1)

    return jnp.concatenate([enc(r), enc(col)], axis=-1)


def _add_body(x_ref, p_ref, o_ref):
    o_ref[0] = x_ref[0] + p_ref[...]


def _add_pos(x, pe):
    b, l, d = x.shape
    tl = 512
    return pl.pallas_call(
        _add_body,
        grid=(l // tl, b),
        in_specs=[pl.BlockSpec((1, tl, d), lambda i, j: (j, i, 0)),
                  pl.BlockSpec((tl, d), lambda i, j: (i, 0))],
        out_specs=pl.BlockSpec((1, tl, d), lambda i, j: (j, i, 0)),
        out_shape=jax.ShapeDtypeStruct(x.shape, F32),
        compiler_params=_cparams(("arbitrary", "arbitrary")),
        name="add_pos",
    )(x, pe)


def _norm_proj_body(x_ref, g_ref, sh_ref, sc_ref, *refs, n_w):
    h = _adaln(x_ref[0], g_ref[...], sh_ref[0], sc_ref[0]).astype(BF16)
    for w_ref, o_ref in zip(refs[:n_w], refs[n_w:]):
        o_ref[0] = _dot(h, w_ref[...]).astype(o_ref.dtype)


def _norm_proj(x, g, shift, scale, ws, dtypes):
    b, l, d = x.shape
    tl = min(l, 256)
    bm = shift.shape[0]
    in_specs = [
        pl.BlockSpec((1, tl, d), lambda i, j: (i, j, 0)),
        _const_spec((1, d)),
        _mod_spec(bm),
        _mod_spec(bm),
    ] + [_const_spec(w.shape) for w in ws]
    return pl.pallas_call(
        functools.partial(_norm_proj_body, n_w=len(ws)),
        grid=(b, l // tl),
        in_specs=in_specs,
        out_specs=[pl.BlockSpec((1, tl, w.shape[1]), lambda i, j: (i, j, 0)) for w in ws],
        out_shape=[jax.ShapeDtypeStruct((b, l, w.shape[1]), dt) for w, dt in zip(ws, dtypes)],
        compiler_params=_cparams(("arbitrary", "arbitrary")),
        name="norm_proj",
    )(x, g.reshape(1, d), shift, scale, *ws)


def _conv_body(x_ref, xm_ref, xp_ref, w_ref, b_ref, o_ref, *, n_l2, n_qscale, hd):
    i = pl.program_id(1)
    j = pl.program_id(2)
    x = x_ref[0].astype(F32)
    r, tc = x.shape
    row = lax.broadcasted_iota(I32, (r, tc), 0)
    halo = xm_ref.shape[1]
    prev_row = jnp.where(i == 0, 0.0, xm_ref[0].astype(F32)[halo - 1:halo, :])
    next_row = jnp.where(i == pl.num_programs(1) - 1, 0.0, xp_ref[0].astype(F32)[0:1, :])
    x_prev = jnp.where(row == 0, prev_row, pltpu.roll(x, 1, axis=0))
    x_next = jnp.where(row == r - 1, next_row, pltpu.roll(x, r - 1, axis=0))
    w = w_ref[...]
    y = _silu(x_prev * w[0:1] + x * w[1:2] + x_next * w[2:3] + b_ref[...])
    if n_l2 == 0:
        o_ref[0] = y.astype(o_ref.dtype)
        return

    @pl.when(j >= n_l2)
    def _():
        o_ref[0] = y.astype(o_ref.dtype)

    @pl.when(j < n_l2)
    def _():
        qs = jnp.where(j < n_qscale, hd ** -0.5, 1.0)
        for k in range(tc // hd):
            yk = y[:, k * hd:(k + 1) * hd]
            ss = jnp.sum(yk * yk, axis=-1, keepdims=True)
            o_ref[0, :, k * hd:(k + 1) * hd] = (yk * (lax.rsqrt(ss + EPS) * qs)).astype(o_ref.dtype)


def _conv_silu(x, w, bias, n_l2=0, n_qscale=0, hd=GDN_DK):
    b, l, c = x.shape
    r = min(l, 512)
    tc = 512
    halo = 2 * SUBLANES
    rs = r // halo
    last = l // halo - 1
    return pl.pallas_call(
        functools.partial(_conv_body, n_l2=n_l2, n_qscale=n_qscale, hd=hd),
        grid=(b, l // r, c // tc),
        in_specs=[
            pl.BlockSpec((1, r, tc), lambda bi, i, j: (bi, i, j)),
            pl.BlockSpec((1, halo, tc), lambda bi, i, j: (bi, jnp.maximum(i * rs - 1, 0), j)),
            pl.BlockSpec((1, halo, tc), lambda bi, i, j: (bi, jnp.minimum((i + 1) * rs, last), j)),
            pl.BlockSpec((3, tc), lambda bi, i, j: (0, j)),
            pl.BlockSpec((1, tc), lambda bi, i, j: (0, j)),
        ],
        out_specs=pl.BlockSpec((1, r, tc), lambda bi, i, j: (bi, i, j)),
        out_shape=jax.ShapeDtypeStruct(x.shape, x.dtype),
        compiler_params=_cparams(("arbitrary", "arbitrary", "arbitrary")),
        name="conv_silu",
    )(x, x, x, w, bias.reshape(1, c))


def _chunk_masks(fwd):
    ii = lax.broadcasted_iota(I32, (CHUNK, CHUNK), 0)
    jj = lax.broadcasted_iota(I32, (CHUNK, CHUNK), 1)
    incl = (ii >= jj) if fwd else (ii <= jj)
    strict = (ii > jj) if fwd else (ii < jj)
    return incl, strict


def _cumsum_chunk(incl, v):
    return jnp.dot(incl.astype(F32), v, precision=HIGHEST, preferred_element_type=F32)


def _softplus(x):
    return jnp.maximum(x, 0.0) + jnp.log1p(jnp.exp(-jnp.abs(x)))


def _ssd_chunks(steps, bias, a, spread_ref, diag_ref):
    units = []
    for si, dirs in enumerate(steps):
      for d, (x_ref, b_ref, c_ref, dt_ref, r0, s_ref, y_ref, col0, fwd) in enumerate(dirs):
        incl, _ = _chunk_masks(fwd)
        rows = pl.ds(r0, CHUNK)
        bm_t = b_ref[0, rows, :].astype(F32).T.astype(BF16)
        cm = c_ref[0, rows, :]
        dt = _softplus(dt_ref[0, rows, :] + bias)
        acs = _cumsum_chunk(incl, dt * a)
        acs_t = acs.T
        tot = acs[CHUNK - 1:CHUNK] if fwd else acs[0:1]
        cols = jnp.concatenate([dt, jnp.exp(tot - acs), jnp.exp(acs),
                                jnp.broadcast_to(jnp.exp(tot), (SUBLANES, LANES))], axis=0)
        units.append(dict(
            x_ref=x_ref, y_ref=y_ref, s_ref=s_ref, bm_t=bm_t, cm=cm, cols=cols, spread=spread_ref[d],
            rows=rows, step=si, cb=_dot(cm, bm_t),
            lms=[jnp.exp(jnp.where(incl, acs[:, col0 + k:col0 + k + 1] - acs_t[col0 + k:col0 + k + 1, :], -jnp.inf))
                 for k in range(SSD_HPG)]))
    wide = []
    for u in units:
        h1 = u["cols"].astype(BF16)
        r1 = u["cols"] - h1.astype(F32)
        h2 = r1.astype(BF16)
        h3 = (r1 - h2.astype(F32)).astype(BF16)
        wide.append(_dot(h1, u["spread"]) + _dot(h2, u["spread"]) + _dot(h3, u["spread"]))
    q = CHUNK
    xdt = [u["x_ref"][0, u["rows"], :] * w[0:q] for u, w in zip(units, wide)]
    m_cat = [jnp.concatenate([u["cb"] * lm for lm in u["lms"]], axis=1).astype(BF16) for u in units]
    y_in = [_dot(m, jnp.tile(xd.astype(BF16), (SSD_HPG, 1)) * diag_ref[...]) for m, xd in zip(m_cat, xdt)]
    for si in range(len(steps)):
        sel = [i for i, u in enumerate(units) if u["step"] == si]
        ss = [units[i]["s_ref"][...] for i in sel]
        y_x = [_dot(units[i]["cm"], s.astype(BF16)) * wide[i][2 * q:3 * q] for i, s in zip(sel, ss)]
        for i, yx in zip(sel, y_x):
            u = units[i]
            u["y_ref"][0, u["rows"], :] = (y_in[i] + yx).astype(BF16)
        new = [s * wide[i][3 * q:3 * q + 1] + _dot(units[i]["bm_t"], (xdt[i] * wide[i][q:2 * q]).astype(BF16))
               for i, s in zip(sel, ss)]
        for i, s in zip(sel, new):
            units[i]["s_ref"][...] = s


def _load_states(states, s0_ref):
    for k, s in enumerate(states):
        s[...] = jnp.zeros_like(s) if s0_ref is None else s0_ref[0, k]


def _store_states(out_ref, states):
    for k, s in enumerate(states):
        out_ref[0, k] = s[...]


def _ssd_body(*refs, has_state):
    (xf, bf, cf, dtf, xb, bb, cbk, dtb, bias_ref, alog_ref, spread_ref, diag_ref), refs = refs[:12], refs[12:]
    s0f = s0b = None
    if has_state:
        (s0f, s0b), refs = refs[:2], refs[2:]
    yf, yb, sfo, sbo, sf, sb = refs
    c = pl.program_id(2)
    p = SSD_HEAD_DIM

    @pl.when(c == 0)
    def _():
        for s, s0 in ((sf, s0f), (sb, s0b)):
            for k in range(SSD_HPG):
                s[:, k * p:(k + 1) * p] = jnp.zeros((SSD_STATE, p), F32) if s0 is None else s0[0, k].T

    bias = bias_ref[0]
    a = -jnp.exp(alog_ref[0])
    n_sub = xf.shape[1] // CHUNK
    steps = [[(xf, bf, cf, dtf, s * CHUNK, sf, yf, 0, True),
              (xb, bb, cbk, dtb, (n_sub - 1 - s) * CHUNK, sb, yb, SSD_HPG, False)] for s in range(n_sub)]
    _ssd_chunks(steps, bias, a, spread_ref, diag_ref)

    @pl.when(c == pl.num_programs(2) - 1)
    def _():
        for s, out in ((sf, sfo), (sb, sbo)):
            for k in range(SSD_HPG):
                out[0, k] = s[:, k * p:(k + 1) * p].T


def _ssd_scan(xbc, dt, bias, alog, s0f=None, s0b=None):
    b, l, _ = xbc.shape
    rows = SSD_STEP_CHUNKS * CHUNK
    nc = l // rows
    gw = SSD_HPG * SSD_HEAD_DIM
    boff = SSD_INNER // SSD_STATE
    coff = boff + SSD_GROUPS
    has_state = s0f is not None

    def seq_specs(ci):
        return [
            pl.BlockSpec((1, rows, gw), lambda bi, g, c: (bi, ci(c), g)),
            pl.BlockSpec((1, rows, SSD_STATE), lambda bi, g, c: (bi, ci(c), boff + g)),
            pl.BlockSpec((1, rows, SSD_STATE), lambda bi, g, c: (bi, ci(c), coff + g)),
            pl.BlockSpec((1, rows, LANES), lambda bi, g, c: (bi, ci(c), g)),
        ]

    fw = lambda c: c
    bw = lambda c: nc - 1 - c
    st_spec = pl.BlockSpec((1, SSD_HPG, SSD_HEAD_DIM, SSD_STATE), lambda bi, g, c: (bi, g, 0, 0))
    par_spec = pl.BlockSpec((1, 1, LANES), lambda bi, g, c: (g, 0, 0))
    head_of_lane = jnp.arange(gw) // SSD_HEAD_DIM
    lane_col = jnp.arange(LANES)[:, None]
    spread = jnp.stack([lane_col == head_of_lane[None, :], lane_col == head_of_lane[None, :] + SSD_HPG]).astype(BF16)
    diag = (jnp.arange(SSD_HPG * CHUNK)[:, None] // CHUNK == head_of_lane[None, :]).astype(BF16)
    in_specs = seq_specs(fw) + seq_specs(bw) + [par_spec, par_spec, _const_spec(spread.shape), _const_spec(diag.shape)]
    args = [xbc, xbc, xbc, dt, xbc, xbc, xbc, dt, bias, alog, spread, diag]
    if has_state:
        in_specs += [st_spec, st_spec]
        args += [s0f, s0b]
    st_shape = jax.ShapeDtypeStruct((b, SSD_HEADS, SSD_HEAD_DIM, SSD_STATE), F32)
    y_shape = jax.ShapeDtypeStruct((b, l, SSD_INNER), BF16)
    return pl.pallas_call(
        functools.partial(_ssd_body, has_state=has_state),
        grid=(b, SSD_GROUPS, nc),
        in_specs=in_specs,
        out_specs=[
            pl.BlockSpec((1, rows, gw), lambda bi, g, c: (bi, c, g)),
            pl.BlockSpec((1, rows, gw), lambda bi, g, c: (bi, nc - 1 - c, g)),
            st_spec, st_spec,
        ],
        out_shape=[y_shape, y_shape, st_shape, st_shape],
        scratch_shapes=[pltpu.VMEM((SSD_STATE, gw), F32)] * 2,
        compiler_params=_cparams(("arbitrary", "arbitrary", "arbitrary")),
        name="ssd_scan",
    )(*args)


def _ssd_out_body(yf_ref, yb_ref, xs_ref, z_ref, d_ref, ng_ref, w_ref, x_ref, gate_ref, o_ref):
    y = yf_ref[0].astype(F32) + yb_ref[0].astype(F32) + xs_ref[0] * d_ref[...]
    y = y * _silu(z_ref[0].astype(F32))
    y = y * lax.rsqrt(jnp.mean(y * y, axis=-1, keepdims=True) + EPS) * ng_ref[...]
    o_ref[0] = x_ref[0] + gate_ref[0] * _dot(y.astype(BF16), w_ref[...])


def _ssd_out(yf, yb, xbc, z, d_skip, norm_g, w_out, x, gate):
    b, l, d = x.shape
    tl = min(l, 256)
    inner = SSD_INNER
    tok = lambda n: pl.BlockSpec((1, tl, n), lambda i, j: (i, j, 0))
    return pl.pallas_call(
        _ssd_out_body,
        grid=(b, l // tl),
        in_specs=[tok(inner), tok(inner), tok(inner), tok(inner),
                  _const_spec((1, inner)), _const_spec((1, inner)), _const_spec(w_out.shape),
                  tok(d), _mod_spec(gate.shape[0])],
        out_specs=tok(d),
        out_shape=jax.ShapeDtypeStruct(x.shape, F32),
        compiler_params=_cparams(("arbitrary", "arbitrary")),
        name="ssd_out",
    )(yf, yb, xbc, z, d_skip, norm_g, w_out, x, gate)


def _mm(a, b):
    return _dot(a.astype(BF16), b.astype(BF16))


def _mm_nt(a, b):
    return _dot_nt(a.astype(BF16), b.astype(BF16))


INV_BASE_LOG2 = 3


def _unit_tri_inverses(ms):
    ii = lax.broadcasted_iota(I32, (CHUNK, CHUNK), 0)
    jj = lax.broadcasted_iota(I32, (CHUNK, CHUNK), 1)
    eye = jnp.where(ii == jj, 1.0, 0.0)
    base = (ii >> INV_BASE_LOG2) == (jj >> INV_BASE_LOG2)
    pws = [jnp.where(base, m, 0.0) for m in ms]
    xs = [eye - d for d in pws]
    for _ in range(INV_BASE_LOG2 - 1):
        pws = [_mm(pw, pw) for pw in pws]
        xs = [x + _mm(x, pw) for x, pw in zip(xs, pws)]
    for lg in range(INV_BASE_LOG2, int(math.log2(CHUNK))):
        off_diag = ((ii >> (lg + 1)) == (jj >> (lg + 1))) & ((ii >> lg) != (jj >> lg))
        cxs = [_mm(jnp.where(off_diag, m, 0.0), x) for m, x in zip(ms, xs)]
        xs = [x - _mm(x, cx) for x, cx in zip(xs, cxs)]
    return xs


def _gdn_chunks(steps, bias, neg_a):
    dk, dv = GDN_DK, GDN_DV
    units = []
    for si, dirs in enumerate(steps):
      for q_ref, k_ref, v_ref, ab_ref, r0, states, o_ref, off, fwd in dirs:
        incl, strict = _chunk_masks(fwd)
        rows = pl.ds(r0, CHUNK)
        q = q_ref[0, rows, :]
        k = k_ref[0, rows, :]
        v = v_ref[0, rows, :]
        ab = ab_ref[0, rows, :]
        beta_all = jax.nn.sigmoid(ab)
        gc = _cumsum_chunk(incl, neg_a * _softplus(ab + bias))
        gc_t = gc.T
        tot = gc[CHUNK - 1:CHUNK] if fwd else gc[0:1]
        e_gc = jnp.exp(gc)
        e_rest = jnp.exp(tot - gc)
        g_last = jnp.exp(tot)
        for h in range(GDN_HPG):
            cb = off + h
            ca = 2 * GDN_HPG + off + h
            kh = k[:, h * dk:(h + 1) * dk]
            beta = beta_all[:, cb:cb + 1]
            units.append(dict(
                incl=incl, strict=strict, kh=kh, kb=kh * beta,
                qh=q[:, h * dk:(h + 1) * dk], vb=v[:, h * dv:(h + 1) * dv] * beta,
                decay=jnp.exp(jnp.where(incl, gc[:, ca:ca + 1] - gc_t[ca:ca + 1, :], -jnp.inf)),
                e_gc=e_gc[:, ca:ca + 1], e_rest=e_rest[:, ca:ca + 1], g_last=g_last[:, ca:ca + 1],
                s_ref=states[h], o_ref=o_ref, rows=rows, cols=slice(h * dv, (h + 1) * dv), step=si))
    ms = [jnp.where(u["strict"], _mm_nt(u["kb"], u["kh"]) * u["decay"], 0.0) for u in units]
    ts = _unit_tri_inverses(ms)
    us = [_mm(t, u["vb"]) for t, u in zip(ts, units)]
    ws = [_mm(t, u["kb"] * u["e_gc"]) for t, u in zip(ts, units)]
    aqk = [jnp.where(u["incl"], _mm_nt(u["qh"], u["kh"]) * u["decay"], 0.0) for u in units]
    for si in range(len(steps)):
        sel = [i for i, u in enumerate(units) if u["step"] == si]
        ss = [units[i]["s_ref"][...] for i in sel]
        vn = [us[i] - _mm(ws[i], s) for i, s in zip(sel, ss)]
        os_ = [_mm(units[i]["qh"] * units[i]["e_gc"], s) + _mm(aqk[i], v) for i, s, v in zip(sel, ss, vn)]
        for i, o in zip(sel, os_):
            u = units[i]
            u["o_ref"][0, u["rows"], u["cols"]] = o.astype(BF16)
        new = [s * units[i]["g_last"] + _mm((units[i]["kh"] * units[i]["e_rest"]).T, v)
               for i, s, v in zip(sel, ss, vn)]
        for i, s in zip(sel, new):
            units[i]["s_ref"][...] = s


def _gdn_body(*refs, has_state):
    (qf, kf, vf, abf, qb, kb, vb, abb, bias_ref, alog_ref), refs = refs[:10], refs[10:]
    s0f = s0b = None
    if has_state:
        (s0f, s0b), refs = refs[:2], refs[2:]
    (of, ob, sfo, sbo), states = refs[:4], refs[4:]
    sf, sb = states[:GDN_HPG], states[GDN_HPG:]
    c = pl.program_id(2)

    @pl.when(c == 0)
    def _():
        _load_states(sf, s0f)
        _load_states(sb, s0b)

    bias = bias_ref[0]
    neg_a = -jnp.exp(alog_ref[0])
    n_sub = qf.shape[1] // CHUNK
    steps = [[(qf, kf, vf, abf, s * CHUNK, sf, of, 0, True),
              (qb, kb, vb, abb, (n_sub - 1 - s) * CHUNK, sb, ob, GDN_HPG, False)] for s in range(n_sub)]
    _gdn_chunks(steps, bias, neg_a)

    @pl.when(c == pl.num_programs(2) - 1)
    def _():
        _store_states(sfo, sf)
        _store_states(sbo, sb)


def _gdn_scan(qkv, ab, bias, alog, s0f=None, s0b=None):
    b, l, _ = qkv.shape
    rows = GDN_STEP_CHUNKS * CHUNK
    nc = l // rows
    qw = GDN_HPG * GDN_DK
    vw = GDN_HPG * GDN_DV
    koff = GDN_QK // qw
    voff = 2 * GDN_QK // vw
    has_state = s0f is not None

    def seq_specs(ci):
        return [
            pl.BlockSpec((1, rows, qw), lambda bi, g, c: (bi, ci(c), g)),
            pl.BlockSpec((1, rows, qw), lambda bi, g, c: (bi, ci(c), koff + g)),
            pl.BlockSpec((1, rows, vw), lambda bi, g, c: (bi, ci(c), voff + g)),
            pl.BlockSpec((1, rows, LANES), lambda bi, g, c: (bi, ci(c), g)),
        ]

    fw = lambda c: c
    bw = lambda c: nc - 1 - c
    st_spec = pl.BlockSpec((1, GDN_HPG, GDN_DK, GDN_DV), lambda bi, g, c: (bi, g, 0, 0))
    par_spec = pl.BlockSpec((1, 1, LANES), lambda bi, g, c: (g, 0, 0))
    in_specs = seq_specs(fw) + seq_specs(bw) + [par_spec, par_spec]
    args = [qkv, qkv, qkv, ab, qkv, qkv, qkv, ab, bias, alog]
    if has_state:
        in_specs += [st_spec, st_spec]
        args += [s0f, s0b]
    st_shape = jax.ShapeDtypeStruct((b, GDN_HEADS, GDN_DK, GDN_DV), F32)
    o_shape = jax.ShapeDtypeStruct((b, l, GDN_VW), BF16)
    return pl.pallas_call(
        functools.partial(_gdn_body, has_state=has_state),
        grid=(b, GDN_GROUPS, nc),
        in_specs=in_specs,
        out_specs=[
            pl.BlockSpec((1, rows, vw), lambda bi, g, c: (bi, c, g)),
            pl.BlockSpec((1, rows, vw), lambda bi, g, c: (bi, nc - 1 - c, g)),
            st_spec, st_spec,
        ],
        out_shape=[o_shape, o_shape, st_shape, st_shape],
        scratch_shapes=[pltpu.VMEM((GDN_DK, GDN_DV), F32)] * (2 * GDN_HPG),
        compiler_params=_cparams(("arbitrary", "arbitrary", "arbitrary")),
        name="gdn_scan",
    )(*args)


def _gdn_out_body(of_ref, ob_ref, z_ref, ng_ref, w_ref, x_ref, gate_ref, o_ref, y_ref):
    dv = GDN_DV
    for h in range(GDN_HEADS):
        sl = slice(h * dv, (h + 1) * dv)
        o = of_ref[0, :, sl].astype(F32) + ob_ref[0, :, sl].astype(F32)
        o = o * lax.rsqrt(jnp.mean(o * o, axis=-1, keepdims=True) + EPS) * ng_ref[...]
        y_ref[:, sl] = (o * _silu(z_ref[0, :, sl].astype(F32))).astype(BF16)
    o_ref[0] = x_ref[0] + gate_ref[0] * _dot(y_ref[...], w_ref[...])


def _gdn_out(of, ob, z, norm_g, w_out, x, gate):
    b, l, d = x.shape
    tl = min(l, 256)
    tok = lambda n: pl.BlockSpec((1, tl, n), lambda i, j: (i, j, 0))
    return pl.pallas_call(
        _gdn_out_body,
        grid=(b, l // tl),
        in_specs=[tok(GDN_VW), tok(GDN_VW), tok(GDN_VW),
                  _const_spec((1, GDN_DV)), _const_spec(w_out.shape),
                  tok(d), _mod_spec(gate.shape[0])],
        out_specs=tok(d),
        out_shape=jax.ShapeDtypeStruct(x.shape, F32),
        scratch_shapes=[pltpu.VMEM((tl, GDN_VW), BF16)],
        compiler_params=_cparams(("arbitrary", "arbitrary")),
        name="gdn_out",
    )(of, ob, z, norm_g, w_out, x, gate)


def _topk_chunks(sc, idc, k, fill):
    vals, idxs = [], []
    for _ in range(k):
        lv = list(zip(sc, idc))
        while len(lv) > 1:
            nxt = []
            for a in range(0, len(lv) - 1, 2):
                (va, ia), (vb, ib) = lv[a], lv[a + 1]
                nxt.append((jnp.maximum(va, vb), jnp.where(vb > va, ib, ia)))
            if len(lv) % 2:
                nxt.append(lv[-1])
            lv = nxt
        v8, i8 = lv[0]
        m = jnp.max(v8, axis=0, keepdims=True)
        idx = jnp.min(jnp.where(v8 == m, i8, fill), axis=0, keepdims=True)
        vals.append(m)
        idxs.append(idx)
        sc = [jnp.where(i == idx, -jnp.inf, v) for v, i in zip(sc, idc)]
    return jnp.concatenate(vals, axis=0), jnp.concatenate(idxs, axis=0).astype(I32)


def _topk_rows(s, k):
    n, t = s.shape
    row = lax.broadcasted_iota(I32, (SUBLANES, t), 0).astype(F32)
    sc = [s[r:r + SUBLANES] for r in range(0, n, SUBLANES)]
    return _topk_chunks(sc, [row + float(r) for r in range(0, n, SUBLANES)], k, float(n - 1))


def _topk_pairs(v0, v1):
    k = PEER_TOPK
    t = v0.shape[1]
    row = lax.broadcasted_iota(I32, (SUBLANES, t), 0).astype(F32)
    sc = [v0[0:1] + v1[0:SUBLANES], v0[0:1] + v1[SUBLANES:]]
    idc = [row, row + float(SUBLANES)]
    for a in range(1, SUBLANES):
        s = v0[a:a + 1] + v1[0:SUBLANES]
        sc.append(jnp.where(row < float(k // (a + 1)), s, -jnp.inf))
        idc.append(row + float(a * k))
    sc.append(v0[SUBLANES:] + v1[0:1])
    idc.append(row * float(k) + float(SUBLANES * k))
    return _topk_chunks(sc, idc, k, float(k * k))


def _pick_rows(sel, table):
    out = jnp.zeros(sel.shape, table.dtype)
    for a in range(table.shape[0]):
        out = jnp.where(sel == a, table[a:a + 1], out)
    return out


def _peer_select_body(x_ref, g_ref, sh_ref, sc_ref, wqh_ref, wql_ref, kh_ref, kl_ref,
                      h_ref, eid_ref, gate_ref):
    h = _adaln(x_ref[0], g_ref[...], sh_ref[0], sc_ref[0])
    for m in range(ROW_VREGS):
        h_ref[0, pl.ds(m, h.shape[0], stride=ROW_VREGS), :] = h[:, m * LANES:(m + 1) * LANES]
    hh, hl = _split_bf16(h)
    q = _dot(hh, wqh_ref[...]) + _dot(hh, wql_ref[...]) + _dot(hl, wqh_ref[...])
    eids, gates = [], []
    for hd in range(PEER_HEADS):
        sv, si = [], []
        for s in range(2):
            c0 = hd * PEER_DKEY + s * PEER_HALF
            qh, ql = _split_bf16(q[:, c0:c0 + PEER_HALF])
            kh = kh_ref[s, hd]
            st = _dot_nt(kh, qh) + _dot_nt(kh, ql) + _dot_nt(kl_ref[s, hd], qh)
            v, i = _topk_rows(st, PEER_TOPK)
            sv.append(v)
            si.append(i)
        cv, ci = _topk_pairs(sv[0], sv[1])
        shift = int(math.log2(PEER_TOPK))
        i1 = _pick_rows(ci >> shift, si[0])
        i2 = _pick_rows(ci & (PEER_TOPK - 1), si[1])
        eids.append(i1 * N_KEYS + i2)
        e = jnp.exp(cv - cv[0:1])
        gates.append(e / jnp.sum(e, axis=0, keepdims=True))
    eid_ref[0] = jnp.concatenate(eids, axis=0).astype(F32).T.astype(I32)
    gate_ref[0] = jnp.concatenate(gates, axis=0).T


def _peer_select(x, g, shift, scale, wq_hi, wq_lo, k_hi, k_lo):
    b, l, d = x.shape
    tl = min(l, 256)
    bm = shift.shape[0]
    tok = lambda n: pl.BlockSpec((1, tl, n), lambda i, j: (i, j, 0))
    return pl.pallas_call(
        _peer_select_body,
        grid=(b, l // tl),
        in_specs=[tok(d), _const_spec((1, d)), _mod_spec(bm), _mod_spec(bm),
                  _const_spec(wq_hi.shape), _const_spec(wq_lo.shape),
                  _const_spec(k_hi.shape), _const_spec(k_lo.shape)],
        out_specs=[pl.BlockSpec((1, tl * ROW_VREGS, LANES), lambda i, j: (i, j, 0)),
                   tok(PEER_SLOTS), tok(PEER_SLOTS)],
        out_shape=[jax.ShapeDtypeStruct((b, l * ROW_VREGS, LANES), F32),
                   jax.ShapeDtypeStruct((b, l, PEER_SLOTS), I32),
                   jax.ShapeDtypeStruct((b, l, PEER_SLOTS), F32)],
        compiler_params=_cparams(("arbitrary", "arbitrary")),
        name="peer_select",
    )(x, g.reshape(1, d), shift, scale, wq_hi, wq_lo, k_hi, k_lo)


def _pack_body(t_ref, o_ref):
    t = t_ref[...]
    half = t.shape[1] // 2
    lo = lax.bitcast_convert_type(t[:, :half].astype(BF16).astype(F32), jnp.uint32)
    hi = lax.bitcast_convert_type(t[:, half:].astype(BF16).astype(F32), jnp.uint32)
    word = (hi & jnp.uint32(0xFFFF0000)) | (lo >> 16)
    o_ref[...] = lax.bitcast_convert_type(word, I32)


def _pack_table(tab):
    e, d = tab.shape
    r = 512
    out = pl.pallas_call(
        _pack_body,
        grid=(e // r,),
        in_specs=[pl.BlockSpec((r, d), lambda i: (i, 0))],
        out_specs=pl.BlockSpec((r, d // 2), lambda i: (i, 0)),
        out_shape=jax.ShapeDtypeStruct((e, d // 2), I32),
        compiler_params=_cparams(("arbitrary",)),
        name="pack_table",
    )(tab)
    return out.reshape(e, HALF_ROWS, LANES)


def _unpack_words(w):
    lo = lax.bitcast_convert_type(w << 16, F32)
    hi = lax.bitcast_convert_type(w & jnp.int32(-65536), F32)
    return lo, hi


PEER_LAG = 2


def _peer_u_body(eid_ref, hr_ref, gate_ref, tab_ref, act_ref, ps_a, ps_b, acc_ref, *, tb):
    ones = jnp.ones((LANES, LANES), BF16)
    lane = lax.broadcasted_iota(I32, (PEER_SLOTS, tb), 1)

    def gather(t, ps_ref):
        xr = hr_ref[0, pl.ds(pl.multiple_of(t * ROW_VREGS, ROW_VREGS), ROW_VREGS), :]
        x_lo = xr[0:HALF_ROWS]
        x_hi = xr[HALF_ROWS:ROW_VREGS]
        for j in range(PEER_SLOTS):
            lo, hi = _unpack_words(tab_ref[eid_ref[0, t, j]])
            ps_ref[j * HALF_ROWS:(j + 1) * HALF_ROWS, :] = lo * x_lo + hi * x_hi

    def reduce(ps_ref, tok):
        part = ps_ref[pl.ds(0, PEER_SLOTS, stride=HALF_ROWS), :]
        for s in range(1, HALF_ROWS):
            part = part + ps_ref[pl.ds(s, PEER_SLOTS, stride=HALF_ROWS), :]
        p_hi, p_lo = _split_bf16(part)
        sums = _dot(p_hi, ones) + _dot(p_lo, ones)
        acc_ref[...] = jnp.where(lane == tok, sums, acc_ref[...])

    ps_a[...] = jnp.zeros_like(ps_a)
    ps_b[...] = jnp.zeros_like(ps_b)
    acc_ref[...] = jnp.zeros_like(acc_ref)

    def pair(i, carry):
        t = i * PEER_LAG
        reduce(ps_a, t - PEER_LAG)
        reduce(ps_b, t - PEER_LAG + 1)
        gather(t, ps_a)
        gather(t + 1, ps_b)
        return carry

    lax.fori_loop(0, tb // PEER_LAG, pair, 0)
    reduce(ps_a, tb - PEER_LAG)
    reduce(ps_b, tb - PEER_LAG + 1)
    a = acc_ref[...].T
    act_ref[0] = 0.5 * a * (1.0 + lax.erf(a * (2.0 ** -0.5))) * gate_ref[0]


def _peer_v_body(eid_ref, act_ref, x_ref, gate_ref, eye_ref, tab_ref, o_ref, wb_a, wb_b, res_ref, *, tb):
    ones = jnp.ones((PEER_SLOTS, LANES), BF16)

    def spread(t, wb_ref):
        a = act_ref[0, pl.ds(t, 1), :]
        diag = jnp.where(eye_ref[...] != 0.0, a, 0.0)
        wb_ref[...] = _dot(diag.astype(BF16), ones)

    def combine(t, wb_ref):
        n_acc = 2
        acc_lo = [jnp.zeros((HALF_ROWS, LANES), F32) for _ in range(n_acc)]
        acc_hi = [jnp.zeros((HALF_ROWS, LANES), F32) for _ in range(n_acc)]
        for j in range(PEER_SLOTS):
            lo, hi = _unpack_words(tab_ref[eid_ref[0, t, j]])
            wj = wb_ref[j:j + 1, :]
            acc_lo[j % n_acc] = acc_lo[j % n_acc] + lo * wj
            acc_hi[j % n_acc] = acc_hi[j % n_acc] + hi * wj
        rows = pl.ds(pl.multiple_of(t * ROW_VREGS, ROW_VREGS), ROW_VREGS)
        res_ref[rows, :] = jnp.concatenate([acc_lo[0] + acc_lo[1], acc_hi[0] + acc_hi[1]], axis=0)

    spread(0, wb_a)

    def pair(i, carry):
        t = i * PEER_LAG
        spread(t + 1, wb_b)
        combine(t, wb_a)
        spread(jnp.minimum(t + PEER_LAG, tb - 1), wb_a)
        combine(t + 1, wb_b)
        return carry

    lax.fori_loop(0, tb // PEER_LAG, pair, 0)
    for m in range(ROW_VREGS):
        cols = slice(m * LANES, (m + 1) * LANES)
        o_ref[0, :, cols] = x_ref[0, :, cols] + gate_ref[0, :, cols] * res_ref[pl.ds(m, tb, stride=ROW_VREGS), :]


def _peer_retrieve(x, hr, eid, gate_w, mod_gate, tab_u, tab_v):
    b, l, d = x.shape
    tb = 128
    tok = pl.BlockSpec((1, tb, PEER_SLOTS), lambda i, j: (i, j, 0))
    eid_spec = pl.BlockSpec((1, tb, PEER_SLOTS), lambda i, j: (i, j, 0), memory_space=pltpu.SMEM)
    row_spec = pl.BlockSpec((1, tb * ROW_VREGS, LANES), lambda i, j: (i, j, 0))
    x_spec = pl.BlockSpec((1, tb, d), lambda i, j: (i, j, 0))
    tab_spec = pl.BlockSpec(memory_space=pltpu.VMEM)
    assert tb == LANES
    eye = jnp.eye(PEER_SLOTS, dtype=F32)
    part = pltpu.VMEM((PEER_SLOTS * HALF_ROWS, LANES), F32)
    act = pl.pallas_call(
        functools.partial(_peer_u_body, tb=tb),
        grid=(b, l // tb),
        in_specs=[eid_spec, row_spec, tok, tab_spec],
        out_specs=tok,
        out_shape=jax.ShapeDtypeStruct((b, l, PEER_SLOTS), F32),
        scratch_shapes=[part, part, pltpu.VMEM((PEER_SLOTS, tb), F32)],
        compiler_params=_cparams(("arbitrary", "arbitrary")),
        name="peer_u",
    )(eid, hr, gate_w, tab_u)
    spread = pltpu.VMEM((PEER_SLOTS, LANES), F32)
    return pl.pallas_call(
        functools.partial(_peer_v_body, tb=tb),
        grid=(b, l // tb),
        in_specs=[eid_spec, tok, x_spec, _mod_spec(mod_gate.shape[0]), _const_spec(eye.shape), tab_spec],
        out_specs=x_spec,
        out_shape=jax.ShapeDtypeStruct(x.shape, F32),
        scratch_shapes=[spread, spread, pltpu.VMEM((tb * ROW_VREGS, LANES), F32)],
        compiler_params=_cparams(("arbitrary", "arbitrary")),
        name="peer_v",
    )(eid, act, x, mod_gate, eye, tab_v)


def _final_norm_body(x_ref, g_ref, o_ref):
    x = x_ref[0]
    o_ref[0] = x * lax.rsqrt(jnp.mean(x * x, axis=-1, keepdims=True) + EPS) * g_ref[...]


def _final_norm(x, g):
    b, l, d = x.shape
    tl = min(l, 512)
    return pl.pallas_call(
        _final_norm_body,
        grid=(b, l // tl),
        in_specs=[pl.BlockSpec((1, tl, d), lambda i, j: (i, j, 0)), _const_spec((1, d))],
        out_specs=pl.BlockSpec((1, tl, d), lambda i, j: (i, j, 0)),
        out_shape=jax.ShapeDtypeStruct(x.shape, F32),
        compiler_params=_cparams(("arbitrary", "arbitrary")),
        name="final_norm",
    )(x, g.reshape(1, d))


def _group_blocked(cols, per_group, groups):
    d = cols[0].shape[0]
    blocks = []
    for g in range(groups):
        parts = [c[:, g * per_group:(g + 1) * per_group] for c in cols]
        used = per_group * len(cols)
        blocks.append(jnp.concatenate(parts + [jnp.zeros((d, LANES - used), cols[0].dtype)], axis=1))
    return jnp.concatenate(blocks, axis=1)


def _group_blocked_vec(vecs, per_group, groups):
    rows = []
    for g in range(groups):
        parts = [jnp.zeros((per_group,), F32) if v is None else v[g * per_group:(g + 1) * per_group]
                 for v in vecs]
        used = per_group * len(vecs)
        rows.append(jnp.concatenate(parts + [jnp.zeros((LANES - used,), F32)]))
    return jnp.stack(rows).reshape(groups, 1, LANES)


def _ssd_params(w_in, conv_w, conv_b, dt_bias, a_log, d_skip, norm_g, w_out):
    dt0 = SSD_INNER + SSD_CONV_DIM
    w_dt = _group_blocked([w_in[:, dt0:dt0 + SSD_HEADS], w_in[:, dt0 + SSD_HEADS:]], SSD_HPG, SSD_GROUPS)
    return dict(
        ws=[w_in[:, :SSD_INNER].astype(BF16), w_in[:, SSD_INNER:dt0].astype(BF16), w_dt.astype(BF16)],
        conv_w=conv_w, conv_b=conv_b,
        bias=_group_blocked_vec([dt_bias[0], dt_bias[1]], SSD_HPG, SSD_GROUPS),
        alog=_group_blocked_vec([a_log[0], a_log[1]], SSD_HPG, SSD_GROUPS),
        d_skip=jnp.repeat(d_skip, SSD_HEAD_DIM).reshape(1, SSD_INNER),
        norm_g=norm_g.reshape(1, SSD_INNER),
        w_out=w_out.astype(BF16),
    )


def _gdn_params(w_in, conv_w, dt_bias, a_log, norm_g, w_out):
    ab0 = GDN_CONV_DIM + GDN_VW
    h = GDN_HEADS
    ab = [w_in[:, ab0 + i * h:ab0 + (i + 1) * h] for i in range(4)]
    return dict(
        ws=[w_in[:, :GDN_CONV_DIM].astype(BF16), w_in[:, GDN_CONV_DIM:ab0].astype(BF16),
            _group_blocked(ab, GDN_HPG, GDN_GROUPS).astype(BF16)],
        conv_w=conv_w,
        bias=_group_blocked_vec([None, None, dt_bias[0], dt_bias[1]], GDN_HPG, GDN_GROUPS),
        alog=_group_blocked_vec([None, None, a_log[0], a_log[1]], GDN_HPG, GDN_GROUPS),
        norm_g=norm_g.reshape(1, GDN_DV),
        w_out=w_out.astype(BF16),
    )


def _peer_params(w_q, keys, u_tab, v_tab):
    wq_hi, wq_lo = _split_bf16(w_q)
    k_hi, k_lo = _split_bf16(keys)
    return dict(wq_hi=wq_hi, wq_lo=wq_lo, k_hi=k_hi, k_lo=k_lo,
                tab_u=_pack_table(u_tab), tab_v=_pack_table(v_tab))


def _ssd_layer(x, m, g, p, s0f=None, s0b=None):
    z, xbc, dt = _norm_proj(x, g, m[0], m[1], p["ws"], [BF16, BF16, F32])
    xbc = _conv_silu(xbc, p["conv_w"], p["conv_b"])
    yf, yb, sf, sb = _ssd_scan(xbc, dt, p["bias"], p["alog"], s0f, s0b)
    x = _ssd_out(yf, yb, xbc, z, p["d_skip"], p["norm_g"], p["w_out"], x, m[2])
    return x, sf, sb


def _gdn_layer(x, m, g, p, s0f=None, s0b=None):
    qkv, z, ab = _norm_proj(x, g, m[0], m[1], p["ws"], [BF16, BF16, F32])
    qkv = _conv_silu(qkv, p["conv_w"], jnp.zeros((GDN_CONV_DIM,), F32),
                     n_l2=2 * GDN_QK // 512, n_qscale=GDN_QK // 512)
    of, ob, sf, sb = _gdn_scan(qkv, ab, p["bias"], p["alog"], s0f, s0b)
    x = _gdn_out(of, ob, z, p["norm_g"], p["w_out"], x, m[2])
    return x, sf, sb


def _peer_layer(x, m, g, p):
    h, eid, gate_w = _peer_select(x, g, m[3], m[4], p["wq_hi"], p["wq_lo"], p["k_hi"], p["k_lo"])
    return _peer_retrieve(x, h, eid, gate_w, m[5], p["tab_u"], p["tab_v"])


def kernel(x_prompt, x_sample, c, state_ssd_fwd, state_ssd_bwd, state_gdn_fwd, state_gdn_bwd, c_ctx, w_mod, b_mod, norm_mix_g, norm_ffn_g, ssd_w_in, ssd_conv_w, ssd_conv_b, ssd_dt_bias, ssd_a_log, ssd_d, ssd_norm_g, ssd_w_out, gdn_w_in, gdn_conv_w, gdn_dt_bias, gdn_a_log, gdn_norm_g, gdn_w_out, peer_w_q, peer_keys, peer_u, peer_v, final_norm_g):
    d = D_MODEL
    nb = c.shape[0]
    rows = 2 * SUBLANES
    c_all = jnp.zeros((rows, d), F32).at[0].set(c_ctx).at[1:1 + nb].set(c)
    mod = _modulation(c_all, w_mod, b_mod)
    xp = x_prompt
    xs = _add_pos(x_sample, _grid_pos_embed(x_sample.shape[1]))
    ssd_f, ssd_b, gdn_f, gdn_b = [], [], [], []
    for i in range(DEPTH):
        mp = [mod[i, 0:1, k * d:(k + 1) * d].reshape(1, 1, d) for k in range(6)]
        ms = [mod[i, 1:1 + nb, k * d:(k + 1) * d].reshape(nb, 1, d) for k in range(6)]
        j = i // 2
        if i % 2 == 0:
            p = _ssd_params(ssd_w_in[j], ssd_conv_w[j], ssd_conv_b[j], ssd_dt_bias[j], ssd_a_log[j],
                            ssd_d[j], ssd_norm_g[j], ssd_w_out[j])
            xp, sf, sb = _ssd_layer(xp, mp, norm_mix_g[i], p)
            xs, _, _ = _ssd_layer(xs, ms, norm_mix_g[i], p, state_ssd_fwd[:, j], state_ssd_bwd[:, j])
            ssd_f.append(sf)
            ssd_b.append(sb)
        else:
            p = _gdn_params(gdn_w_in[j], gdn_conv_w[j], gdn_dt_bias[j], gdn_a_log[j], gdn_norm_g[j], gdn_w_out[j])
            xp, sf, sb = _gdn_layer(xp, mp, norm_mix_g[i], p)
            xs, _, _ = _gdn_layer(xs, ms, norm_mix_g[i], p, state_gdn_fwd[:, j], state_gdn_bwd[:, j])
            gdn_f.append(sf)
            gdn_b.append(sb)
        pp = _peer_params(peer_w_q[i], peer_keys[i], peer_u[i], peer_v[i])
        xp = _peer_layer(xp, mp, norm_ffn_g[i], pp)
        xs = _peer_layer(xs, ms, norm_ffn_g[i], pp)
    y_prompt = _final_norm(xp, final_norm_g)
    y_sample = _final_norm(xs, final_norm_g)
    return (y_prompt, y_sample, jnp.stack(ssd_f, axis=1), jnp.stack(ssd_b, axis=1),
            jnp.stack(gdn_f, axis=1), jnp.stack(gdn_b, axis=1))
```
